```python
import jax, jax.numpy as jnp
from jax import lax
import numpy as np

D_MODEL = 1024
BATCH = 32
SEQ = 2048
DEPTH = 1

HEAD_DIM = 64
ROPE_THETA = 10000.0
A_Q_HEADS = 16
A_KV_HEADS = 4
A_GQA = A_Q_HEADS // A_KV_HEADS
A_WINDOW = 128
B_GROUPS = ((128, 1), (512, 4), (2048, 16))
B_HEADS_PER_GROUP = 4
ATTN_BLOCK = 128
A_Q_W = A_Q_HEADS * HEAD_DIM
A_KV_W = A_KV_HEADS * HEAD_DIM
B_OUT_W = B_HEADS_PER_GROUP * HEAD_DIM
B_W = len(B_GROUPS) * B_OUT_W
N_BRANCH = 2
OFF_AQ = 0
OFF_AK = OFF_AQ + A_Q_W
OFF_AV = OFF_AK + A_KV_W
OFF_BQ = OFF_AV + A_KV_W
OFF_BK = OFF_BQ + B_W
OFF_BV = OFF_BK + B_W
OFF_G = OFF_BV + B_W
IN_W = OFF_G + N_BRANCH * D_MODEL
N_EXPERTS = 32
TOP_K = 4
D_EXPERT = D_MODEL
SWIGLU_LIMIT = 7.0
SWIGLU_ALPHA = 1.702
MOE_BLOCK = 512
LN_EPS = 1e-5
DEEPNORM_ALPHA = (2 * DEPTH) ** 0.25
DEEPNORM_BETA = (8 * DEPTH) ** -0.25

kernel_name = 'hybrid_swa_sink_dilated_moe_deepnorm'


def layer_norm(x, g, b):
    xf = x.astype(jnp.float32)
    mu = jnp.mean(xf, axis=-1, keepdims=True)
    var = jnp.mean(jnp.square(xf - mu), axis=-1, keepdims=True)
    y = (xf - mu) * lax.rsqrt(var + LN_EPS)
    return (y * g.astype(jnp.float32) + b.astype(jnp.float32)).astype(x.dtype)


def rope(x, pos):
    half = x.shape[-1] // 2
    inv_freq = ROPE_THETA ** (-jnp.arange(0, x.shape[-1], 2, dtype=jnp.float32) / x.shape[-1])
    ang = pos[:, None] * inv_freq[None, :]
    cos = jnp.cos(ang)[None, :, None, :]
    sin = jnp.sin(ang)[None, :, None, :]
    xf = x.astype(jnp.float32)
    x1, x2 = xf[..., :half], xf[..., half:]
    return jnp.concatenate([x1 * cos - x2 * sin, x2 * cos + x1 * sin], axis=-1).astype(x.dtype)


def banded_attention(q, k, v, n_back, sink):
    n, L, hk, g, hd = q.shape
    blk = ATTN_BLOCK
    nb = -(-L // blk)
    pad = nb * blk - L
    if pad:
        q = jnp.pad(q, ((0, 0), (0, pad), (0, 0), (0, 0), (0, 0)))
        k = jnp.pad(k, ((0, 0), (0, pad), (0, 0), (0, 0)))
        v = jnp.pad(v, ((0, 0), (0, pad), (0, 0), (0, 0)))
    qb = q.reshape(n, nb, blk, hk, g, hd)

    def with_prev(t):
        tb = t.reshape(n, nb, blk, hk, hd)
        prev = jnp.pad(tb, ((0, 0), (1, 0), (0, 0), (0, 0), (0, 0)))[:, :-1]
        return jnp.concatenate([prev, tb], axis=2)

    kk, vv = with_prev(k), with_prev(v)
    s = jnp.einsum('nbqhgd,nbkhd->nbhgqk', qb, kk,
                   preferred_element_type=jnp.float32) * (hd ** -0.5)
    qi = jnp.arange(blk)[:, None] + blk
    kj = jnp.arange(2 * blk)[None, :]
    dist = qi - kj
    band = (dist >= 0) & (dist <= n_back)
    key_pos = jnp.arange(nb)[:, None, None] * blk + kj[None] - blk
    valid = band[None] & (key_pos >= 0)
    s = jnp.where(valid[None, :, None, None], s, -jnp.inf)
    m = jnp.max(s, axis=-1)
    if sink is not None:
        sk = sink.astype(jnp.float32)[:, :, None]
        m = jnp.maximum(m, sk)
    p = jnp.exp(s - m[..., None])
    l = jnp.sum(p, axis=-1)
    if sink is not None:
        l = l + jnp.exp(sk - m)
    o = jnp.einsum('nbhgqk,nbkhd->nbqhgd', p.astype(v.dtype), vv,
                   preferred_element_type=jnp.float32)
    l_t = jnp.moveaxis(l, -1, 2)
    o = o / l_t[..., None]
    lse = jnp.moveaxis(m, -1, 2) + jnp.log(l_t)
    o = o.reshape(n, nb * blk, hk, g, hd)[:, :L]
    lse = lse.reshape(n, nb * blk, hk, g)[:, :L]
    return o.astype(q.dtype), lse


def to_class(t, d):
    b, s = t.shape[:2]
    t = t.reshape(b, s // d, d, *t.shape[2:])
    t = jnp.moveaxis(t, 2, 1)
    return t.reshape(b * d, s // d, *t.shape[3:])


def from_class(t, d, b):
    t = t.reshape(b, d, t.shape[1], *t.shape[2:])
    t = jnp.moveaxis(t, 1, 2)
    return t.reshape(b, -1, *t.shape[3:])


def hybrid_mixer(x, w_in, sinks, w_branch_a, w_branch_b, w_out):
    b, s, _ = x.shape
    pos = jnp.arange(s, dtype=jnp.float32)
    proj = jnp.einsum('bsd,de->bse', x, w_in)

    def heads(lo, w):
        return proj[..., lo:lo + w].reshape(b, s, w // HEAD_DIM, HEAD_DIM)

    qa = rope(heads(OFF_AQ, A_Q_W), pos).reshape(b, s, A_KV_HEADS, A_GQA, HEAD_DIM)
    ka = rope(heads(OFF_AK, A_KV_W), pos)
    va = heads(OFF_AV, A_KV_W)
    oa, _ = banded_attention(qa, ka, va, A_WINDOW - 1, sinks.reshape(A_KV_HEADS, A_GQA))
    oa = oa.reshape(b, s, A_Q_W)

    qb = rope(heads(OFF_BQ, B_W), pos)
    kb = rope(heads(OFF_BK, B_W), pos)
    vb = heads(OFF_BV, B_W)
    outs, lses = [], []
    for gi, (window, dil) in enumerate(B_GROUPS):
        sl = slice(gi * B_HEADS_PER_GROUP, (gi + 1) * B_HEADS_PER_GROUP)
        qc = to_class(qb[:, :, sl], dil)[:, :, :, None]
        kc = to_class(kb[:, :, sl], dil)
        vc = to_class(vb[:, :, sl], dil)
        o, lse = banded_attention(qc, kc, vc, window // dil, None)
        outs.append(from_class(o[:, :, :, 0], dil, b))
        lses.append(from_class(lse[:, :, :, 0], dil, b))
    wts = jax.nn.softmax(jnp.stack(lses, axis=0), axis=0)
    ob = jnp.sum(wts[..., None] * jnp.stack(outs, axis=0).astype(jnp.float32), axis=0)
    ob = ob.astype(x.dtype).reshape(b, s, B_OUT_W)

    gates = jax.nn.sigmoid(proj[..., OFF_G:].reshape(b, s, N_BRANCH, D_MODEL)
                           .astype(jnp.float32)).astype(x.dtype)
    ya = jnp.einsum('bsi,id->bsd', oa, w_branch_a)
    yb = jnp.einsum('bsi,id->bsd', ob, w_branch_b)
    merged = gates[:, :, 0] * ya + gates[:, :, 1] * yb
    return jnp.einsum('bsd,de->bse', merged, w_out)


def moe_ffn(x, w_router, b_router, w_gate_up, b_gate_up, w_down, b_down):
    b, s, d = x.shape
    t = b * s
    xt = x.reshape(t, d)
    logits = jnp.einsum('td,de->te', xt, w_router,
                        preferred_element_type=jnp.float32) + b_router.astype(jnp.float32)
    top_v, top_e = lax.top_k(logits, TOP_K)
    gate_w = jax.nn.softmax(top_v, axis=-1)
    n = t * TOP_K
    flat_e = top_e.reshape(n)
    flat_w = gate_w.reshape(n)
    order = jnp.argsort(flat_e)
    sorted_e = flat_e[order]
    sorted_tok = (order // TOP_K).astype(jnp.int32)
    sorted_w = flat_w[order]
    counts = jnp.bincount(flat_e, length=N_EXPERTS)
    padded = (counts + MOE_BLOCK - 1) // MOE_BLOCK * MOE_BLOCK
    pad_end = jnp.cumsum(padded)
    pad_start = pad_end - padded
    start = jnp.cumsum(counts) - counts
    dest = pad_start[sorted_e] + (jnp.arange(n) - start[sorted_e])
    nblk = -(-(n + N_EXPERTS * (MOE_BLOCK - 1)) // MOE_BLOCK)
    rows = nblk * MOE_BLOCK
    buf_tok = jnp.full((rows,), t, jnp.int32).at[dest].set(sorted_tok)
    buf_w = jnp.zeros((rows,), jnp.float32).at[dest].set(sorted_w)
    blk_e = jnp.minimum(jnp.searchsorted(pad_end, jnp.arange(nblk) * MOE_BLOCK, side='right'),
                        N_EXPERTS - 1)
    x_pad = jnp.concatenate([xt, jnp.zeros((1, d), xt.dtype)], axis=0)

    def expert_block(args):
        tok, w, e = args
        xb = x_pad[tok]
        gu = xb @ w_gate_up[e] + b_gate_up[e]
        gate = jnp.minimum(gu[:, :D_EXPERT], SWIGLU_LIMIT)
        up = jnp.clip(gu[:, D_EXPERT:], -SWIGLU_LIMIT, SWIGLU_LIMIT)
        h = (up + 1.0) * gate * jax.nn.sigmoid(SWIGLU_ALPHA * gate)
        y = h @ w_down[e] + b_down[e]
        return y * w[:, None].astype(y.dtype)

    y = lax.map(expert_block, (buf_tok.reshape(nblk, MOE_BLOCK),
                               buf_w.reshape(nblk, MOE_BLOCK), blk_e))
    out = jnp.zeros((t + 1, d), y.dtype).at[buf_tok].add(y.reshape(rows, d))[:t]
    return out.reshape(b, s, d)


def setup_inputs(seed: int = 0) -> dict:
    key = jax.random.key(seed)
    ks = jax.random.split(key, 16)
    f32 = jnp.float32

    def nrm(k, shape):
        return jax.random.normal(k, shape, f32)

    col_scale = jnp.concatenate([
        jnp.ones((A_Q_W + A_KV_W,), f32), jnp.full((A_KV_W,), DEEPNORM_BETA, f32),
        jnp.ones((2 * B_W,), f32), jnp.full((B_W,), DEEPNORM_BETA, f32),
        jnp.ones((N_BRANCH * D_MODEL,), f32)])
    return {
        'x': nrm(ks[0], (BATCH, SEQ, D_MODEL)),
        'w_in': nrm(ks[1], (DEPTH, D_MODEL, IN_W)) * (D_MODEL ** -0.5) * col_scale,
        'attn_sinks': 0.5 * nrm(ks[2], (DEPTH, A_Q_HEADS)),
        'w_branch_a': nrm(ks[3], (DEPTH, A_Q_W, D_MODEL)) * (A_Q_W ** -0.5),
        'w_branch_b': nrm(ks[4], (DEPTH, B_OUT_W, D_MODEL)) * (B_OUT_W ** -0.5),
        'w_out': nrm(ks[5], (DEPTH, D_MODEL, D_MODEL)) * (D_MODEL ** -0.5) * DEEPNORM_BETA,
        'ln1_g': 1.0 + 0.02 * nrm(ks[6], (DEPTH, D_MODEL)),
        'ln1_b': 0.02 * nrm(ks[7], (DEPTH, D_MODEL)),
        'w_router': nrm(ks[8], (DEPTH, D_MODEL, N_EXPERTS)) * (D_MODEL ** -0.5),
        'b_router': 0.01 * nrm(ks[9], (DEPTH, N_EXPERTS)),
        'w_gate_up': nrm(ks[10], (DEPTH, N_EXPERTS, D_MODEL, 2 * D_EXPERT)) * (D_MODEL ** -0.5),
        'b_gate_up': 0.02 * nrm(ks[11], (DEPTH, N_EXPERTS, 2 * D_EXPERT)),
        'w_down': nrm(ks[12], (DEPTH, N_EXPERTS, D_EXPERT, D_MODEL)) * (D_EXPERT ** -0.5) * DEEPNORM_BETA,
        'b_down': 0.02 * nrm(ks[13], (DEPTH, N_EXPERTS, D_MODEL)),
        'ln2_g': 1.0 + 0.02 * nrm(ks[14], (DEPTH, D_MODEL)),
        'ln2_b': 0.02 * nrm(ks[15], (DEPTH, D_MODEL)),
    }


def reference(x, w_in, attn_sinks, w_branch_a, w_branch_b, w_out, ln1_g, ln1_b,
              w_router, b_router, w_gate_up, b_gate_up, w_down, b_down, ln2_g, ln2_b):
    h = x
    for l in range(DEPTH):
        mix = hybrid_mixer(h, w_in[l], attn_sinks[l], w_branch_a[l], w_branch_b[l], w_out[l])
        h = layer_norm(DEEPNORM_ALPHA * h + mix, ln1_g[l], ln1_b[l])
        ffn = moe_ffn(h, w_router[l], b_router[l], w_gate_up[l], b_gate_up[l], w_down[l], b_down[l])
        h = layer_norm(DEEPNORM_ALPHA * h + ffn, ln2_g[l], ln2_b[l])
    return h
```

```python
import functools

import jax
import jax.numpy as jnp
from jax import lax
from jax.experimental import pallas as pl
from jax.experimental.pallas import tpu as pltpu

F32 = jnp.float32
BF16 = jnp.bfloat16
I32 = jnp.int32

D_MODEL = 1024
HEAD_DIM = 64
ROPE_THETA = 10000.0
A_Q_W = 1024
A_KV_W = 256
A_KV_HEADS = 4
A_GQA = 4
A_WINDOW = 128
B_GROUPS = ((128, 1), (512, 4), (2048, 16))
B_GROUP_W = 256
B_HEADS = 4
B_W = 768
OFF_AK = A_Q_W
OFF_AV = OFF_AK + A_KV_W
OFF_BQ = OFF_AV + A_KV_W
OFF_BK = OFF_BQ + B_W
OFF_BV = OFF_BK + B_W
OFF_G = OFF_BV + B_W
N_EXPERTS = 32
TOP_K = 4
D_EXPERT = 1024
SWIGLU_LIMIT = 7.0
SWIGLU_ALPHA = 1.702
LN_EPS = 1e-5
DEEPNORM_ALPHA = 2.0 ** 0.25

LANES = 128
ATTN_BLOCK = 128
SEQ_TILE = 512
ROW_TILE = 512
MOE_BLOCK = 512
FFN_CHUNK = 512
COMBINE_TILE = 128
A_SEG_W = A_Q_W + 2 * A_KV_W
B_SEG_W = 3 * B_GROUP_W
QKV_W = A_SEG_W + 3 * B_SEG_W
ROUTER_PAD = LANES
NEG = -1e30
VMEM_LIMIT = 48 * 1024 * 1024


def _qkv_rope_kernel(x_ref, w_ref, cos_ref, sin_ref, a_ref, b0_ref, b1_ref, b2_ref, scr_ref):
    tm = x_ref.shape[0]
    xb = x_ref[...].astype(BF16)
    cos = cos_ref[...]
    sin = sin_ref[...]
    lane = lax.broadcasted_iota(I32, (tm, LANES), 1)
    first_half = (lane % HEAD_DIM) < (HEAD_DIM // 2)

    def rope(a):
        partner = jnp.where(first_half, pltpu.roll(a, LANES - 32, 1), pltpu.roll(a, 32, 1))
        return a * cos + partner * sin

    def segment(col0, width, n_rope_chunks):
        acc = jnp.dot(xb, w_ref[:, col0:col0 + width], preferred_element_type=F32)
        pieces = []
        for c in range(width // LANES):
            a = acc[:, c * LANES:(c + 1) * LANES]
            pieces.append(rope(a) if c < n_rope_chunks else a)
        return pieces

    a_ref[...] = jnp.concatenate(segment(0, A_SEG_W, (A_Q_W + A_KV_W) // LANES), axis=1).astype(BF16)
    b0_ref[...] = jnp.concatenate(segment(A_SEG_W, B_SEG_W, 2 * B_GROUP_W // LANES), axis=1).astype(BF16)
    n_chunks = B_SEG_W // LANES
    for out_ref, col0, dil in ((b1_ref, A_SEG_W + B_SEG_W, 4), (b2_ref, A_SEG_W + 2 * B_SEG_W, 16)):
        for k, piece in enumerate(segment(col0, B_SEG_W, 2 * B_GROUP_W // LANES)):
            scr_ref[k] = piece
        for c in range(dil):
            rows = [scr_ref[k, pl.ds(c, tm // dil, stride=dil), :] for k in range(n_chunks)]
            out_ref[c] = jnp.concatenate(rows, axis=1).astype(BF16)


def _qkv_rope(x, w_qkv, cos_t, sin_t):
    b, s, d = x.shape
    tm = SEQ_TILE
    grid = (b, s // tm)
    return pl.pallas_call(
        _qkv_rope_kernel,
        grid=grid,
        in_specs=[
            pl.BlockSpec((None, tm, d), lambda bi, si: (bi, si, 0)),
            pl.BlockSpec((d, QKV_W), lambda bi, si: (0, 0)),
            pl.BlockSpec((tm, LANES), lambda bi, si: (si, 0)),
            pl.BlockSpec((tm, LANES), lambda bi, si: (si, 0)),
        ],
        out_specs=[
            pl.BlockSpec((None, tm, A_SEG_W), lambda bi, si: (bi, si, 0)),
            pl.BlockSpec((None, tm, B_SEG_W), lambda bi, si: (bi, si, 0)),
            pl.BlockSpec((None, 4, tm // 4, B_SEG_W), lambda bi, si: (bi, 0, si, 0)),
            pl.BlockSpec((None, 16, tm // 16, B_SEG_W), lambda bi, si: (bi, 0, si, 0)),
        ],
        out_shape=[
            jax.ShapeDtypeStruct((b, s, A_SEG_W), BF16),
            jax.ShapeDtypeStruct((b, s, B_SEG_W), BF16),
            jax.ShapeDtypeStruct((b, 4, s // 4, B_SEG_W), BF16),
            jax.ShapeDtypeStruct((b, 16, s // 16, B_SEG_W), BF16),
        ],
        scratch_shapes=[pltpu.VMEM((B_SEG_W // LANES, tm, LANES), F32)],
        compiler_params=pltpu.CompilerParams(
            dimension_semantics=("arbitrary", "arbitrary"), vmem_limit_bytes=VMEM_LIMIT),
        name="qkv_rope",
    )(x, w_qkv, cos_t, sin_t)


def _softmax_unit(q, k, v, mask, sink):
    s = lax.dot_general(q, k, (((1,), (1,)), ((), ())), preferred_element_type=F32)
    s = jnp.where(mask, s, NEG)
    m = jnp.max(s, axis=-1, keepdims=True)
    if sink is not None:
        m = jnp.maximum(m, sink)
    p = jnp.exp(s - m)
    l = jnp.sum(p, axis=-1, keepdims=True)
    if sink is not None:
        l = l + jnp.exp(sink - m)
    num = jnp.dot(p.astype(BF16), v, preferred_element_type=F32)
    return num / l, m + jnp.log(l)


def _head(x, h):
    return x[:, h * HEAD_DIM:(h + 1) * HEAD_DIM]


def _group_heads(q, k, v, mask):
    outs, lses = [], []
    for h in range(B_HEADS):
        o, lse = _softmax_unit(_head(q, h), _head(k, h), _head(v, h), mask, None)
        outs.append(o)
        lses.append(jnp.broadcast_to(lse, o.shape))
    return outs, lses


def _store_head_pairs(ref, lead, rows, per_head):
    for pair in range(B_HEADS // 2):
        ref[(*lead, pair, rows, slice(None))] = jnp.concatenate(per_head[2 * pair:2 * pair + 2], axis=1)


def _attention_kernel(sink_ref, a_cur, a_prev, b0_cur, b0_prev, b1_cur, b1_prev, b2_ref,
                      oa_ref, ob_ref, kva_scr, kvb_scr, o_scr, lse_scr, o2_scr, lse2_scr):
    j = pl.program_id(1)
    blk = ATTN_BLOCK
    nqb = SEQ_TILE // blk

    def band_mask(n_back, first_block, stack=1):
        row = lax.broadcasted_iota(I32, (stack * blk, 2 * blk), 0) % blk
        col = lax.broadcasted_iota(I32, (stack * blk, 2 * blk), 1)
        dist = blk + row - col
        in_band = (dist >= 0) & (dist <= n_back)
        return in_band & ((col >= blk) | jnp.logical_not(first_block))

    @pl.when(j == 0)
    def _():
        r1 = lax.broadcasted_iota(I32, (blk, blk), 0)
        c1 = lax.broadcasted_iota(I32, (blk, blk), 1)
        causal = c1 <= r1

        def class_body(c, carry):
            t = b2_ref[c]
            q, k, v = t[:, :B_GROUP_W], t[:, B_GROUP_W:2 * B_GROUP_W], t[:, 2 * B_GROUP_W:]
            outs, lses = _group_heads(q, k, v, causal)
            rows = pl.ds(c, blk, stride=16)
            _store_head_pairs(o2_scr, (), rows, outs)
            _store_head_pairs(lse2_scr, (), rows, lses)
            return carry

        lax.fori_loop(0, 16, class_body, 0)

    kva_scr[0:blk, :] = a_prev[:, A_Q_W:]
    kva_scr[blk:, :] = a_cur[:, A_Q_W:]
    kvb_scr[0:blk, :] = b0_prev[:, B_GROUP_W:]
    kvb_scr[blk:, :] = b0_cur[:, B_GROUP_W:]

    mask4_rows = lax.broadcasted_iota(I32, (A_GQA * blk, 1), 0) // blk

    def qblock_body(qb, carry):
        r0 = pl.multiple_of(qb * blk, blk)
        first = jnp.logical_and(j == 0, qb == 0)
        mask_a4 = band_mask(A_WINDOW - 1, first, stack=A_GQA)
        qa = a_cur[pl.ds(r0, blk), 0:A_Q_W]
        kva = kva_scr[pl.ds(r0, 2 * blk), :]
        for kvh in range(A_KV_HEADS):
            q4 = jnp.concatenate([_head(qa, kvh * A_GQA + g) for g in range(A_GQA)], axis=0)
            sink = jnp.zeros((A_GQA * blk, 1), F32)
            for g in range(A_GQA):
                sink = jnp.where(mask4_rows == g, sink_ref[kvh * A_GQA + g], sink)
            o4, _ = _softmax_unit(q4, _head(kva, kvh), _head(kva, A_KV_HEADS + kvh), mask_a4, sink)
            o_heads = jnp.concatenate([o4[g * blk:(g + 1) * blk] for g in range(A_GQA)], axis=1)
            oa_ref[pl.ds(r0, blk), kvh * A_GQA * HEAD_DIM:(kvh + 1) * A_GQA * HEAD_DIM] = (
                o_heads.astype(oa_ref.dtype))
        mask_b = band_mask(B_GROUPS[0][0] // B_GROUPS[0][1], first)
        qb0 = b0_cur[pl.ds(r0, blk), 0:B_GROUP_W]
        kvb = kvb_scr[pl.ds(r0, 2 * blk), :]
        outs, lses = _group_heads(qb0, kvb[:, :B_GROUP_W], kvb[:, B_GROUP_W:], mask_b)
        _store_head_pairs(o_scr, (0,), pl.ds(r0, blk), outs)
        _store_head_pairs(lse_scr, (0,), pl.ds(r0, blk), lses)
        return carry

    lax.fori_loop(0, nqb, qblock_body, 0)

    mask_c = band_mask(B_GROUPS[1][0] // B_GROUPS[1][1], j == 0)

    def class4_body(c, carry):
        cur = b1_cur[c]
        prev = b1_prev[c]
        q = cur[:, :B_GROUP_W]
        kv = jnp.concatenate([prev[:, B_GROUP_W:], cur[:, B_GROUP_W:]], axis=0)
        outs, lses = _group_heads(q, kv[:, :B_GROUP_W], kv[:, B_GROUP_W:], mask_c)
        rows = pl.ds(c, blk, stride=4)
        _store_head_pairs(o_scr, (1,), rows, outs)
        _store_head_pairs(lse_scr, (1,), rows, lses)
        return carry

    lax.fori_loop(0, 4, class4_body, 0)

    t0 = pl.multiple_of(j * SEQ_TILE, SEQ_TILE)
    for pair in range(B_HEADS // 2):
        l0, l1, l2 = lse_scr[0, pair], lse_scr[1, pair], lse2_scr[pair, pl.ds(t0, SEQ_TILE), :]
        mx = jnp.maximum(jnp.maximum(l0, l1), l2)
        w0, w1, w2 = jnp.exp(l0 - mx), jnp.exp(l1 - mx), jnp.exp(l2 - mx)
        num = w0 * o_scr[0, pair] + w1 * o_scr[1, pair] + w2 * o2_scr[pair, pl.ds(t0, SEQ_TILE), :]
        ob_ref[:, pair * LANES:(pair + 1) * LANES] = (num / (w0 + w1 + w2)).astype(ob_ref.dtype)


def _attention(sinks, qkv_a, qkv_b0, qkv_b1, qkv_b2):
    b, s, _ = qkv_a.shape
    t = SEQ_TILE
    blk = ATTN_BLOCK
    nq = t // blk
    grid = (b, s // t)
    return pl.pallas_call(
        _attention_kernel,
        grid=grid,
        in_specs=[
            pl.BlockSpec(memory_space=pltpu.SMEM),
            pl.BlockSpec((None, t, A_SEG_W), lambda bi, j: (bi, j, 0)),
            pl.BlockSpec((None, blk, A_SEG_W), lambda bi, j: (bi, jnp.maximum(j * nq - 1, 0), 0)),
            pl.BlockSpec((None, t, B_SEG_W), lambda bi, j: (bi, j, 0)),
            pl.BlockSpec((None, blk, B_SEG_W), lambda bi, j: (bi, jnp.maximum(j * nq - 1, 0), 0)),
            pl.BlockSpec((None, 4, blk, B_SEG_W), lambda bi, j: (bi, 0, j, 0)),
            pl.BlockSpec((None, 4, blk, B_SEG_W), lambda bi, j: (bi, 0, jnp.maximum(j - 1, 0), 0)),
            pl.BlockSpec((None, 16, blk, B_SEG_W), lambda bi, j: (bi, 0, 0, 0)),
        ],
        out_specs=[
            pl.BlockSpec((None, t, A_Q_W), lambda bi, j: (bi, j, 0)),
            pl.BlockSpec((None, t, B_GROUP_W), lambda bi, j: (bi, j, 0)),
        ],
        out_shape=[
            jax.ShapeDtypeStruct((b, s, A_Q_W), BF16),
            jax.ShapeDtypeStruct((b, s, B_GROUP_W), BF16),
        ],
        scratch_shapes=[
            pltpu.VMEM((blk + t, 2 * A_KV_W), BF16),
            pltpu.VMEM((blk + t, 2 * B_GROUP_W), BF16),
            pltpu.VMEM((2, B_GROUP_W // LANES, t, LANES), F32),
            pltpu.VMEM((2, B_GROUP_W // LANES, t, LANES), F32),
            pltpu.VMEM((B_GROUP_W // LANES, s, LANES), F32),
            pltpu.VMEM((B_GROUP_W // LANES, s, LANES), F32),
        ],
        compiler_params=pltpu.CompilerParams(
            dimension_semantics=("arbitrary", "arbitrary"), vmem_limit_bytes=VMEM_LIMIT),
        name="banded_attention",
    )(sinks, qkv_a, qkv_a, qkv_b0, qkv_b0, qkv_b1, qkv_b1, qkv_b2)


def _layer_norm(z, g, b):
    mu = jnp.mean(z, axis=-1, keepdims=True)
    zc = z - mu
    var = jnp.mean(zc * zc, axis=-1, keepdims=True)
    return zc * lax.rsqrt(var + LN_EPS) * g + b


def _merge_kernel(x_ref, oa_ref, ob_ref, wg_ref, wa_ref, wb_ref, wo_ref, wr_ref, br_ref, g_ref, b_ref,
                  h_ref, e_ref, gw_ref):
    x = x_ref[...]
    xb = x.astype(BF16)
    gates = jax.nn.sigmoid(jnp.dot(xb, wg_ref[...], preferred_element_type=F32))
    ya = jnp.dot(oa_ref[...], wa_ref[...], preferred_element_type=F32)
    yb = jnp.dot(ob_ref[...], wb_ref[...], preferred_element_type=F32)
    merged = gates[:, :D_MODEL] * ya + gates[:, D_MODEL:] * yb
    mix = jnp.dot(merged.astype(BF16), wo_ref[...], preferred_element_type=F32)
    h = _layer_norm(DEEPNORM_ALPHA * x + mix, g_ref[...], b_ref[...])
    h_ref[...] = h

    logits = jnp.dot(h.astype(BF16), wr_ref[...], preferred_element_type=F32) + br_ref[...]
    tm = logits.shape[0]
    lane = lax.broadcasted_iota(I32, (tm, ROUTER_PAD), 1)
    lane_f = lane.astype(F32)
    e_out = jnp.zeros((tm, ROUTER_PAD), F32)
    v_out = jnp.full((tm, ROUTER_PAD), NEG, F32)
    lg = logits
    for k in range(TOP_K):
        m = jnp.max(lg, axis=-1, keepdims=True)
        idx = jnp.min(jnp.where(lg == m, lane_f, float(ROUTER_PAD)), axis=-1, keepdims=True)
        e_out = jnp.where(lane == k, idx, e_out)
        v_out = jnp.where(lane == k, m, v_out)
        lg = jnp.where(lane_f == idx, 3.0 * NEG, lg)
    vmax = jnp.max(v_out, axis=-1, keepdims=True)
    ex = jnp.exp(v_out - vmax)
    gw_ref[...] = ex / jnp.sum(ex, axis=-1, keepdims=True)
    e_ref[...] = e_out.astype(I32)


def _merge(x2, oa2, ob2, wg, wa, wb, wo, wr, br, ln_g, ln_b):
    t, d = x2.shape
    tm = ROW_TILE
    full = lambda shape: pl.BlockSpec(shape, lambda i: (0, 0))
    return pl.pallas_call(
        _merge_kernel,
        grid=(t // tm,),
        in_specs=[
            pl.BlockSpec((tm, d), lambda i: (i, 0)),
            pl.BlockSpec((tm, A_Q_W), lambda i: (i, 0)),
            pl.BlockSpec((tm, B_GROUP_W), lambda i: (i, 0)),
            full((d, 2 * d)), full((A_Q_W, d)), full((B_GROUP_W, d)), full((d, d)),
            full((d, ROUTER_PAD)), full((1, ROUTER_PAD)), full((1, d)), full((1, d)),
        ],
        out_specs=[
            pl.BlockSpec((tm, d), lambda i: (i, 0)),
            pl.BlockSpec((tm, ROUTER_PAD), lambda i: (i, 0)),
            pl.BlockSpec((tm, ROUTER_PAD), lambda i: (i, 0)),
        ],
        out_shape=[
            jax.ShapeDtypeStruct((t, d), F32),
            jax.ShapeDtypeStruct((t, ROUTER_PAD), I32),
            jax.ShapeDtypeStruct((t, ROUTER_PAD), F32),
        ],
        compiler_params=pltpu.CompilerParams(
            dimension_semantics=("arbitrary",), vmem_limit_bytes=VMEM_LIMIT),
        name="merge_ln_router",
    )(x2, oa2, ob2, wg, wa, wb, wo, wr, br, ln_g, ln_b)


def _start_row_gather(idx_ref, src_hbm, dst_ref, sem, n_rows):
    def body(r, carry):
        t = idx_ref[0, 0, r]
        pltpu.make_async_copy(src_hbm.at[pl.ds(t, 1)], dst_ref.at[pl.ds(r, 1)], sem).start()
        return carry
    lax.fori_loop(0, n_rows, body, 0, unroll=8)


def _wait_row_gather(src_hbm, dst_ref, sem, n_rows):
    pltpu.make_async_copy(src_hbm.at[pl.ds(0, n_rows)], dst_ref, sem).wait()


def _ffn_kernel(blk_e_ref, nused_ref, tok_cur, tok_next, wrow_ref, h_hbm, wgu_ref, bgu_ref, wdn_ref, bdn_ref,
                y_ref, xbuf, sem):
    i = pl.program_id(0)
    n_used = nused_ref[0]
    slot = i % 2

    @pl.when(i == 0)
    def _():
        _start_row_gather(tok_cur, h_hbm, xbuf.at[0], sem.at[0], MOE_BLOCK)

    @pl.when(i + 1 < n_used)
    def _():
        _start_row_gather(tok_next, h_hbm, xbuf.at[1 - slot], sem.at[1 - slot], MOE_BLOCK)

    @pl.when(i < n_used)
    def _():
        _wait_row_gather(h_hbm, xbuf.at[slot], sem.at[slot], MOE_BLOCK)
        xb = xbuf[slot].astype(BF16)
        acc = jnp.zeros((MOE_BLOCK, D_MODEL), F32)
        for c in range(D_EXPERT // FFN_CHUNK):
            lo = c * FFN_CHUNK
            gate = jnp.dot(xb, wgu_ref[:, lo:lo + FFN_CHUNK], preferred_element_type=F32)
            gate = gate + bgu_ref[:, lo:lo + FFN_CHUNK]
            up = jnp.dot(xb, wgu_ref[:, D_EXPERT + lo:D_EXPERT + lo + FFN_CHUNK], preferred_element_type=F32)
            up = up + bgu_ref[:, D_EXPERT + lo:D_EXPERT + lo + FFN_CHUNK]
            gate = jnp.minimum(gate, SWIGLU_LIMIT)
            up = jnp.clip(up, -SWIGLU_LIMIT, SWIGLU_LIMIT)
            hmid = (up + 1.0) * gate * jax.nn.sigmoid(SWIGLU_ALPHA * gate)
            acc = acc + jnp.dot(hmid.astype(BF16), wdn_ref[lo:lo + FFN_CHUNK, :], preferred_element_type=F32)
        y_ref[...] = (acc + bdn_ref[...]) * wrow_ref[...]

    @pl.when(i >= n_used)
    def _():
        y_ref[...] = jnp.zeros_like(y_ref)


def _expert_ffn(blk_e, n_used, tok_blocks, w_rows, h2, wgu, bgu, wdn, bdn):
    nblk = tok_blocks.shape[0]
    t, d = h2.shape
    grid_spec = pltpu.PrefetchScalarGridSpec(
        num_scalar_prefetch=2,
        grid=(nblk,),
        in_specs=[
            pl.BlockSpec((1, 1, MOE_BLOCK), lambda i, be, nu: (i, 0, 0), memory_space=pltpu.SMEM),
            pl.BlockSpec((1, 1, MOE_BLOCK), lambda i, be, nu: (jnp.minimum(i + 1, nblk - 1), 0, 0),
                         memory_space=pltpu.SMEM),
            pl.BlockSpec((MOE_BLOCK, 1), lambda i, be, nu: (i, 0)),
            pl.BlockSpec(memory_space=pl.ANY),
            pl.BlockSpec((None, d, 2 * D_EXPERT), lambda i, be, nu: (be[i], 0, 0)),
            pl.BlockSpec((None, 1, 2 * D_EXPERT), lambda i, be, nu: (be[i], 0, 0)),
            pl.BlockSpec((None, D_EXPERT, d), lambda i, be, nu: (be[i], 0, 0)),
            pl.BlockSpec((None, 1, d), lambda i, be, nu: (be[i], 0, 0)),
        ],
        out_specs=pl.BlockSpec((MOE_BLOCK, d), lambda i, be, nu: (i, 0)),
        scratch_shapes=[pltpu.VMEM((2, MOE_BLOCK, d), F32), pltpu.SemaphoreType.DMA((2,))],
    )
    return pl.pallas_call(
        _ffn_kernel,
        grid_spec=grid_spec,
        out_shape=jax.ShapeDtypeStruct((nblk * MOE_BLOCK, d), F32),
        compiler_params=pltpu.CompilerParams(
            dimension_semantics=("arbitrary",), vmem_limit_bytes=VMEM_LIMIT),
        name="expert_ffn",
    )(blk_e, n_used, tok_blocks, tok_blocks, w_rows, h2, wgu, bgu, wdn, bdn)


def _combine_kernel(pos_cur, pos_next, y_hbm, h_ref, g_ref, b_ref, out_ref, ybuf, sem):
    i = pl.program_id(0)
    n = pl.num_programs(0)
    slot = i % 2
    rows = TOP_K * COMBINE_TILE

    @pl.when(i == 0)
    def _():
        _start_row_gather(pos_cur, y_hbm, ybuf.at[0], sem.at[0], rows)

    @pl.when(i + 1 < n)
    def _():
        _start_row_gather(pos_next, y_hbm, ybuf.at[1 - slot], sem.at[1 - slot], rows)

    _wait_row_gather(y_hbm, ybuf.at[slot], sem.at[slot], rows)
    ffn = ybuf[slot, 0:COMBINE_TILE, :]
    for k in range(1, TOP_K):
        ffn = ffn + ybuf[slot, k * COMBINE_TILE:(k + 1) * COMBINE_TILE, :]
    out_ref[...] = _layer_norm(DEEPNORM_ALPHA * h_ref[...] + ffn, g_ref[...], b_ref[...])


def _combine(pos_blocks, y_rows, h2, ln_g, ln_b):
    t, d = h2.shape
    n = t // COMBINE_TILE
    rows = TOP_K * COMBINE_TILE
    return pl.pallas_call(
        _combine_kernel,
        grid=(n,),
        in_specs=[
            pl.BlockSpec((1, 1, rows), lambda i: (i, 0, 0), memory_space=pltpu.SMEM),
            pl.BlockSpec((1, 1, rows), lambda i: (jnp.minimum(i + 1, n - 1), 0, 0), memory_space=pltpu.SMEM),
            pl.BlockSpec(memory_space=pl.ANY),
            pl.BlockSpec((COMBINE_TILE, d), lambda i: (i, 0)),
            pl.BlockSpec((1, d), lambda i: (0, 0)),
            pl.BlockSpec((1, d), lambda i: (0, 0)),
        ],
        out_specs=pl.BlockSpec((COMBINE_TILE, d), lambda i: (i, 0)),
        out_shape=jax.ShapeDtypeStruct((t, d), F32),
        scratch_shapes=[pltpu.VMEM((2, rows, d), F32), pltpu.SemaphoreType.DMA((2,))],
        compiler_params=pltpu.CompilerParams(
            dimension_semantics=("arbitrary",), vmem_limit_bytes=VMEM_LIMIT),
        name="combine_ln",
    )(pos_blocks, pos_blocks, y_rows, h2, ln_g, ln_b)


def _route(top_e, gate_w, t):
    n = t * TOP_K
    nblk = -(-(n + N_EXPERTS * (MOE_BLOCK - 1)) // MOE_BLOCK)
    rows = nblk * MOE_BLOCK
    flat_e = top_e.reshape(n)
    onehot = (flat_e[:, None] == jnp.arange(N_EXPERTS, dtype=I32)[None, :]).astype(I32)
    csum = jnp.cumsum(onehot, axis=0)
    counts = csum[-1]
    rank = jnp.sum((csum - onehot) * onehot, axis=1)
    padded = (counts + MOE_BLOCK - 1) // MOE_BLOCK * MOE_BLOCK
    pad_end = jnp.cumsum(padded)
    pad_start = pad_end - padded
    dest = (pad_start[flat_e] + rank).astype(I32)
    tok_rows = jnp.zeros((rows,), I32).at[dest].set(jnp.arange(n, dtype=I32) // TOP_K)
    w_rows = jnp.zeros((rows,), F32).at[dest].set(gate_w.reshape(n))
    blk_e = jnp.minimum(
        jnp.searchsorted(pad_end, jnp.arange(nblk, dtype=I32) * MOE_BLOCK, side="right"), N_EXPERTS - 1
    ).astype(I32)
    n_used = (pad_end[-1] // MOE_BLOCK).astype(I32).reshape(1)
    return dest, tok_rows.reshape(nblk, 1, MOE_BLOCK), w_rows.reshape(rows, 1), blk_e, n_used


def kernel(x, w_in, attn_sinks, w_branch_a, w_branch_b, w_out, ln1_g, ln1_b, w_router, b_router,
           w_gate_up, b_gate_up, w_down, b_down, ln2_g, ln2_b):
    depth = w_in.shape[0]
    b, s, d = x.shape
    t = b * s

    pos = jnp.arange(s, dtype=F32)
    inv_freq = ROPE_THETA ** (-jnp.arange(0, HEAD_DIM, 2, dtype=F32) / HEAD_DIM)
    ang = pos[:, None] * inv_freq[None, :]
    cos32, sin32 = jnp.cos(ang), jnp.sin(ang)
    cos_t = jnp.tile(cos32, (1, LANES // (HEAD_DIM // 2)))
    sin_t = jnp.tile(jnp.concatenate([-sin32, sin32], axis=1), (1, LANES // HEAD_DIM))

    h = x
    for l in range(depth):
        wi = w_in[l]
        qscale = HEAD_DIM ** -0.5
        cols = [wi[:, :A_Q_W] * qscale, wi[:, OFF_AK:OFF_BQ]]
        for g in range(len(B_GROUPS)):
            cols += [wi[:, OFF_BQ + g * B_GROUP_W:OFF_BQ + (g + 1) * B_GROUP_W] * qscale,
                     wi[:, OFF_BK + g * B_GROUP_W:OFF_BK + (g + 1) * B_GROUP_W],
                     wi[:, OFF_BV + g * B_GROUP_W:OFF_BV + (g + 1) * B_GROUP_W]]
        w_qkv = jnp.concatenate(cols, axis=1).astype(BF16)

        qkv_a, qkv_b0, qkv_b1, qkv_b2 = _qkv_rope(h, w_qkv, cos_t, sin_t)
        oa, ob = _attention(attn_sinks[l].astype(F32), qkv_a, qkv_b0, qkv_b1, qkv_b2)

        wr = jnp.zeros((d, ROUTER_PAD), F32).at[:, :N_EXPERTS].set(w_router[l]).astype(BF16)
        br = jnp.full((1, ROUTER_PAD), NEG, F32).at[0, :N_EXPERTS].set(b_router[l].astype(F32))
        h1, top_e, gate_w = _merge(
            h.reshape(t, d), oa.reshape(t, A_Q_W), ob.reshape(t, B_GROUP_W),
            wi[:, OFF_G:].astype(BF16), w_branch_a[l].astype(BF16), w_branch_b[l].astype(BF16),
            w_out[l].astype(BF16), wr, br, ln1_g[l].reshape(1, d), ln1_b[l].reshape(1, d))

        dest, tok_blocks, w_rows, blk_e, n_used = _route(top_e[:, :TOP_K], gate_w[:, :TOP_K], t)
        y_rows = _expert_ffn(
            blk_e, n_used, tok_blocks, w_rows, h1,
            w_gate_up[l].astype(BF16), b_gate_up[l].reshape(N_EXPERTS, 1, 2 * D_EXPERT),
            w_down[l].astype(BF16), b_down[l].reshape(N_EXPERTS, 1, d))

        pos_blocks = dest.reshape(t // COMBINE_TILE, COMBINE_TILE, TOP_K).transpose(0, 2, 1)
        pos_blocks = pos_blocks.reshape(t // COMBINE_TILE, 1, TOP_K * COMBINE_TILE)
        h = _combine(pos_blocks, y_rows, h1, ln2_g[l].reshape(1, d), ln2_b[l].reshape(1, d)).reshape(b, s, d)
    return h
```

```python
import jax
import jax.numpy as jnp
from jax import lax
from jax.experimental import pallas as pl
from jax.experimental.pallas import tpu as pltpu

F32 = jnp.float32
BF16 = jnp.bfloat16
I32 = jnp.int32
U32 = jnp.uint32

D_MODEL = 1024
HEAD_DIM = 64
ROPE_THETA = 10000.0
A_Q_W = 1024
A_KV_W = 256
A_KV_HEADS = 4
A_GQA = 4
A_WINDOW = 128
B_GROUPS = ((128, 1), (512, 4), (2048, 16))
B_GROUP_W = 256
B_HEADS = 4
B_W = 768
OFF_AK = A_Q_W
OFF_AV = OFF_AK + A_KV_W
OFF_BQ = OFF_AV + A_KV_W
OFF_BK = OFF_BQ + B_W
OFF_BV = OFF_BK + B_W
OFF_G = OFF_BV + B_W
N_EXPERTS = 32
TOP_K = 4
D_EXPERT = 1024
SWIGLU_LIMIT = 7.0
SWIGLU_ALPHA = 1.702
LN_EPS = 1e-5
DEEPNORM_ALPHA = 2.0 ** 0.25
LOG2E = 1.4426950408889634

LANES = 128
SUBLANES = 8
ATTN_BLOCK = 128
SEQ_TILE = 512
ROW_TILE = 512
MOE_BLOCK = 512
FFN_CHUNK = 512
RUN_ALIGN = SUBLANES
PERM_ROWS = ROW_TILE * TOP_K + N_EXPERTS * RUN_ALIGN
PERM_CHUNK = 256
A_SEG_W = A_Q_W + 2 * A_KV_W
B_SEG_W = 3 * B_GROUP_W
QKV_W = A_SEG_W + 3 * B_SEG_W
HALF_W = D_MODEL // 2
ROUTER_PAD = LANES
NEG = -1e30
VMEM_LIMIT = 48 * 1024 * 1024


def _qkv_rope_kernel(x_ref, w_ref, cos_ref, sin_ref, a_ref, b0_ref, b1_ref, b2_ref, scr_ref):
    tm = x_ref.shape[0]
    xb = x_ref[...].astype(BF16)
    cos = cos_ref[...]
    sin = sin_ref[...]
    lane = lax.broadcasted_iota(I32, (tm, LANES), 1)
    first_half = (lane % HEAD_DIM) < (HEAD_DIM // 2)

    def rope(a):
        partner = jnp.where(first_half, pltpu.roll(a, LANES - 32, 1), pltpu.roll(a, 32, 1))
        return a * cos + partner * sin

    def segment(col0, width, n_rope_chunks):
        acc = jnp.dot(xb, w_ref[:, col0:col0 + width], preferred_element_type=F32)
        pieces = []
        for c in range(width // LANES):
            a = acc[:, c * LANES:(c + 1) * LANES]
            pieces.append(rope(a) if c < n_rope_chunks else a)
        return pieces

    a_ref[...] = jnp.concatenate(segment(0, A_SEG_W, (A_Q_W + A_KV_W) // LANES), axis=1).astype(BF16)
    b0_ref[...] = jnp.concatenate(segment(A_SEG_W, B_SEG_W, 2 * B_GROUP_W // LANES), axis=1).astype(BF16)
    n_chunks = B_SEG_W // LANES
    for out_ref, col0, dil in ((b1_ref, A_SEG_W + B_SEG_W, 4), (b2_ref, A_SEG_W + 2 * B_SEG_W, 16)):
        for k, piece in enumerate(segment(col0, B_SEG_W, 2 * B_GROUP_W // LANES)):
            scr_ref[k] = piece
        for c in range(dil):
            rows = [scr_ref[k, pl.ds(c, tm // dil, stride=dil), :] for k in range(n_chunks)]
            out_ref[c] = jnp.concatenate(rows, axis=1).astype(BF16)


def _qkv_rope(x, w_qkv, cos_t, sin_t):
    b, s, d = x.shape
    tm = SEQ_TILE
    grid = (b, s // tm)
    return pl.pallas_call(
        _qkv_rope_kernel,
        grid=grid,
        in_specs=[
            pl.BlockSpec((None, tm, d), lambda bi, si: (bi, si, 0)),
            pl.BlockSpec((d, QKV_W), lambda bi, si: (0, 0)),
            pl.BlockSpec((tm, LANES), lambda bi, si: (si, 0)),
            pl.BlockSpec((tm, LANES), lambda bi, si: (si, 0)),
        ],
        out_specs=[
            pl.BlockSpec((None, tm, A_SEG_W), lambda bi, si: (bi, si, 0)),
            pl.BlockSpec((None, tm, B_SEG_W), lambda bi, si: (bi, si, 0)),
            pl.BlockSpec((None, 4, tm // 4, B_SEG_W), lambda bi, si: (bi, 0, si, 0)),
            pl.BlockSpec((None, 16, tm // 16, B_SEG_W), lambda bi, si: (bi, 0, si, 0)),
        ],
        out_shape=[
            jax.ShapeDtypeStruct((b, s, A_SEG_W), BF16),
            jax.ShapeDtypeStruct((b, s, B_SEG_W), BF16),
            jax.ShapeDtypeStruct((b, 4, s // 4, B_SEG_W), BF16),
            jax.ShapeDtypeStruct((b, 16, s // 16, B_SEG_W), BF16),
        ],
        scratch_shapes=[pltpu.VMEM((B_SEG_W // LANES, tm, LANES), F32)],
        compiler_params=pltpu.CompilerParams(
            dimension_semantics=("arbitrary", "arbitrary"), vmem_limit_bytes=VMEM_LIMIT),
        name="qkv_rope",
    )(x, w_qkv, cos_t, sin_t)


def _stacked_heads_unit(q, kmat, vmat, bias, sink):
    blk = q.shape[0]
    group = lax.broadcasted_iota(I32, (1, B_GROUP_W), 1) // HEAD_DIM
    zero = jnp.zeros_like(q)
    qs = jnp.concatenate([jnp.where(group == h, q, zero) for h in range(B_HEADS)], axis=0)
    s = lax.dot_general(qs, kmat, (((1,), (1,)), ((), ())), preferred_element_type=F32) + bias
    m = jnp.max(s, axis=-1, keepdims=True)
    if sink is not None:
        m = jnp.maximum(m, sink)
    p = jnp.exp2(s - m)
    l = jnp.sum(p, axis=-1, keepdims=True)
    if sink is not None:
        l = l + jnp.exp2(sink - m)
    o = jnp.dot(p.astype(BF16), vmat, preferred_element_type=F32)
    scale = 1.0 / l
    lse = m + jnp.log2(l)
    out = jnp.zeros((blk, B_GROUP_W), F32)
    scale_b = jnp.zeros((blk, B_GROUP_W), F32)
    lse_b = jnp.zeros((blk, B_GROUP_W), F32)
    for h in range(B_HEADS):
        out = jnp.where(group == h, o[h * blk:(h + 1) * blk], out)
        scale_b = jnp.where(group == h, scale[h * blk:(h + 1) * blk], scale_b)
        lse_b = jnp.where(group == h, lse[h * blk:(h + 1) * blk], lse_b)
    return out * scale_b, lse_b


def _store_lane_chunks(ref, lead, rows, x):
    for c in range(x.shape[1] // LANES):
        ref[(*lead, c, rows, slice(None))] = x[:, c * LANES:(c + 1) * LANES]


def _attention_kernel(sink_ref, a_cur, a_prev, b0_cur, b0_prev, b1_cur, b1_prev, b2_ref,
                      oa_ref, ob_ref, kexp_scr, vexp_scr, kvb_scr, o_scr, lse_scr, o2_scr, lse2_scr,
                      band_scr, causal_scr):
    j = pl.program_id(1)
    blk = ATTN_BLOCK
    nqb = SEQ_TILE // blk
    stack = B_HEADS * blk

    @pl.when(jnp.logical_and(pl.program_id(0) == 0, j == 0))
    def _():
        row = lax.broadcasted_iota(I32, (stack, 2 * blk), 0) % blk
        col = lax.broadcasted_iota(I32, (stack, 2 * blk), 1)
        dist = blk + row - col
        for w, n_back in enumerate((A_WINDOW - 1, B_GROUPS[0][0] // B_GROUPS[0][1])):
            in_band = (dist >= 0) & (dist <= n_back)
            band_scr[2 * w] = jnp.where(in_band, 0.0, NEG)
            band_scr[2 * w + 1] = jnp.where(in_band & (col >= blk), 0.0, NEG)
        r1 = lax.broadcasted_iota(I32, (stack, blk), 0) % blk
        c1 = lax.broadcasted_iota(I32, (stack, blk), 1)
        causal_scr[...] = jnp.where(c1 <= r1, 0.0, NEG)

    @pl.when(j == 0)
    def _():
        def class_body(c, carry):
            t = b2_ref[c]
            out, lse_b = _stacked_heads_unit(t[:, :B_GROUP_W], t[:, B_GROUP_W:2 * B_GROUP_W],
                                             t[:, 2 * B_GROUP_W:], causal_scr[...], None)
            rows = pl.ds(c, blk, stride=16)
            _store_lane_chunks(o2_scr, (), rows, out)
            _store_lane_chunks(lse2_scr, (), rows, lse_b)
            return carry

        lax.fori_loop(0, 16, class_body, 0, unroll=4)

    for src, r0, r1 in ((a_prev, 0, blk), (a_cur, blk, blk + SEQ_TILE)):
        for dst, col0 in ((kexp_scr, A_Q_W), (vexp_scr, A_Q_W + A_KV_W)):
            heads = src[:, col0:col0 + A_KV_W]
            for kvh in range(A_KV_HEADS):
                head = heads[:, kvh * HEAD_DIM:(kvh + 1) * HEAD_DIM]
                dst[r0:r1, kvh * B_GROUP_W:(kvh + 1) * B_GROUP_W] = jnp.concatenate([head] * A_GQA, axis=1)
    kvb_scr[0:blk, :] = b0_prev[:, B_GROUP_W:]
    kvb_scr[blk:, :] = b0_cur[:, B_GROUP_W:]

    row_block = lax.broadcasted_iota(I32, (stack, 1), 0) // blk

    def qblock_body(qb, carry):
        r0 = pl.multiple_of(qb * blk, blk)
        first = jnp.logical_and(j == 0, qb == 0).astype(I32)
        for kvh in range(A_KV_HEADS):
            lanes = slice(kvh * B_GROUP_W, (kvh + 1) * B_GROUP_W)
            sink = jnp.zeros((stack, 1), F32)
            for g in range(A_GQA):
                sink = jnp.where(row_block == g, sink_ref[kvh * A_GQA + g], sink)
            out, _ = _stacked_heads_unit(a_cur[pl.ds(r0, blk), lanes], kexp_scr[pl.ds(r0, 2 * blk), lanes],
                                         vexp_scr[pl.ds(r0, 2 * blk), lanes], band_scr[first], sink)
            oa_ref[pl.ds(r0, blk), lanes] = out.astype(oa_ref.dtype)
        out, lse_b = _stacked_heads_unit(b0_cur[pl.ds(r0, blk), 0:B_GROUP_W],
                                         kvb_scr[pl.ds(r0, 2 * blk), 0:B_GROUP_W],
                                         kvb_scr[pl.ds(r0, 2 * blk), B_GROUP_W:], band_scr[2 + first], None)
        _store_lane_chunks(o_scr, (0,), pl.ds(r0, blk), out)
        _store_lane_chunks(lse_scr, (0,), pl.ds(r0, blk), lse_b)
        return carry

    lax.fori_loop(0, nqb, qblock_body, 0)

    first_class_block = (j == 0).astype(I32)

    def class4_body(c, carry):
        cur = b1_cur[c]
        prev = b1_prev[c]
        kv = jnp.concatenate([prev[:, B_GROUP_W:], cur[:, B_GROUP_W:]], axis=0)
        out, lse_b = _stacked_heads_unit(cur[:, :B_GROUP_W], kv[:, :B_GROUP_W], kv[:, B_GROUP_W:],
                                         band_scr[2 + first_class_block], None)
        rows = pl.ds(c, blk, stride=4)
        _store_lane_chunks(o_scr, (1,), rows, out)
        _store_lane_chunks(lse_scr, (1,), rows, lse_b)
        return carry

    lax.fori_loop(0, 4, class4_body, 0, unroll=True)

    t0 = pl.multiple_of(j * SEQ_TILE, SEQ_TILE)
    for c in range(B_GROUP_W // LANES):
        l0, l1, l2 = lse_scr[0, c], lse_scr[1, c], lse2_scr[c, pl.ds(t0, SEQ_TILE), :]
        mx = jnp.maximum(jnp.maximum(l0, l1), l2)
        w0, w1, w2 = jnp.exp2(l0 - mx), jnp.exp2(l1 - mx), jnp.exp2(l2 - mx)
        num = w0 * o_scr[0, c] + w1 * o_scr[1, c] + w2 * o2_scr[c, pl.ds(t0, SEQ_TILE), :]
        ob_ref[:, c * LANES:(c + 1) * LANES] = (num / (w0 + w1 + w2)).astype(ob_ref.dtype)


def _attention(sinks, qkv_a, qkv_b0, qkv_b1, qkv_b2):
    b, s, _ = qkv_a.shape
    t = SEQ_TILE
    blk = ATTN_BLOCK
    nq = t // blk
    grid = (b, s // t)
    n_chunks = B_GROUP_W // LANES
    return pl.pallas_call(
        _attention_kernel,
        grid=grid,
        in_specs=[
            pl.BlockSpec(memory_space=pltpu.SMEM),
            pl.BlockSpec((None, t, A_SEG_W), lambda bi, j: (bi, j, 0)),
            pl.BlockSpec((None, blk, A_SEG_W), lambda bi, j: (bi, jnp.maximum(j * nq - 1, 0), 0)),
            pl.BlockSpec((None, t, B_SEG_W), lambda bi, j: (bi, j, 0)),
            pl.BlockSpec((None, blk, B_SEG_W), lambda bi, j: (bi, jnp.maximum(j * nq - 1, 0), 0)),
            pl.BlockSpec((None, 4, blk, B_SEG_W), lambda bi, j: (bi, 0, j, 0)),
            pl.BlockSpec((None, 4, blk, B_SEG_W), lambda bi, j: (bi, 0, jnp.maximum(j - 1, 0), 0)),
            pl.BlockSpec((None, 16, blk, B_SEG_W), lambda bi, j: (bi, 0, 0, 0)),
        ],
        out_specs=[
            pl.BlockSpec((None, t, A_Q_W), lambda bi, j: (bi, j, 0)),
            pl.BlockSpec((None, t, B_GROUP_W), lambda bi, j: (bi, j, 0)),
        ],
        out_shape=[
            jax.ShapeDtypeStruct((b, s, A_Q_W), BF16),
            jax.ShapeDtypeStruct((b, s, B_GROUP_W), BF16),
        ],
        scratch_shapes=[
            pltpu.VMEM((blk + t, A_KV_HEADS * B_GROUP_W), BF16),
            pltpu.VMEM((blk + t, A_KV_HEADS * B_GROUP_W), BF16),
            pltpu.VMEM((blk + t, 2 * B_GROUP_W), BF16),
            pltpu.VMEM((2, n_chunks, t, LANES), F32),
            pltpu.VMEM((2, n_chunks, t, LANES), F32),
            pltpu.VMEM((n_chunks, s, LANES), F32),
            pltpu.VMEM((n_chunks, s, LANES), F32),
            pltpu.VMEM((4, B_HEADS * blk, 2 * blk), F32),
            pltpu.VMEM((B_HEADS * blk, blk), F32),
        ],
        compiler_params=pltpu.CompilerParams(
            dimension_semantics=("arbitrary", "arbitrary"), vmem_limit_bytes=VMEM_LIMIT),
        name="banded_attention",
    )(sinks, qkv_a, qkv_a, qkv_b0, qkv_b0, qkv_b1, qkv_b1, qkv_b2)


def _layer_norm(z, g, b):
    mu = jnp.mean(z, axis=-1, keepdims=True)
    zc = z - mu
    var = jnp.mean(zc * zc, axis=-1, keepdims=True)
    return zc * lax.rsqrt(var + LN_EPS) * g + b


def _merge_kernel(x_ref, oa_ref, ob_ref, wg_ref, wa_ref, wb_ref, wo_ref, wr_ref, br_ref, g_ref, b_ref,
                  h_ref, lp_ref, gw_ref, cnt_ref):
    x = x_ref[...]
    xb = x.astype(BF16)
    gates = jax.nn.sigmoid(jnp.dot(xb, wg_ref[...], preferred_element_type=F32))
    ya = jnp.dot(oa_ref[...], wa_ref[...], preferred_element_type=F32)
    yb = jnp.dot(ob_ref[...], wb_ref[...], preferred_element_type=F32)
    merged = gates[:, :D_MODEL] * ya + gates[:, D_MODEL:] * yb
    mix = jnp.dot(merged.astype(BF16), wo_ref[...], preferred_element_type=F32)
    h = _layer_norm(DEEPNORM_ALPHA * x + mix, g_ref[...], b_ref[...])
    h_ref[...] = h

    logits = jnp.dot(h.astype(BF16), wr_ref[...], preferred_element_type=F32) + br_ref[...]
    tm = logits.shape[0]
    lane = lax.broadcasted_iota(I32, (tm, ROUTER_PAD), 1)
    lane_f = lane.astype(F32)
    v_out = jnp.full((tm, ROUTER_PAD), NEG, F32)
    onehots = []
    lg = logits
    for k in range(TOP_K):
        m = jnp.max(lg, axis=-1, keepdims=True)
        idx = jnp.min(jnp.where(lg == m, lane_f, float(ROUTER_PAD)), axis=-1, keepdims=True)
        chosen = lane_f == idx
        onehots.append(chosen.astype(F32))
        v_out = jnp.where(lane == k, m, v_out)
        lg = jnp.where(chosen, 3.0 * NEG, lg)
    vmax = jnp.max(v_out, axis=-1, keepdims=True)
    ex = jnp.exp(v_out - vmax)
    gw_ref[...] = ex / jnp.sum(ex, axis=-1, keepdims=True)

    routed = onehots[0] + onehots[1] + onehots[2] + onehots[3]
    r_i = lax.broadcasted_iota(I32, (tm, tm), 0)
    c_i = lax.broadcasted_iota(I32, (tm, tm), 1)
    before = jnp.dot((c_i < r_i).astype(BF16), routed.astype(BF16), preferred_element_type=F32)
    cnt = jnp.sum(routed, axis=0, keepdims=True)
    runs = jnp.ceil(cnt * (1.0 / RUN_ALIGN))
    e_r = lax.broadcasted_iota(I32, (ROUTER_PAD, ROUTER_PAD), 0)
    e_c = lax.broadcasted_iota(I32, (ROUTER_PAD, ROUTER_PAD), 1)
    off = jnp.dot(jnp.broadcast_to(runs, (SUBLANES, ROUTER_PAD)).astype(BF16), (e_r < e_c).astype(BF16),
                  preferred_element_type=F32)[0:1] * float(RUN_ALIGN)
    base = before + off
    lp = jnp.zeros((tm, ROUTER_PAD), F32)
    for k in range(TOP_K):
        lp = jnp.where(lane == k, jnp.sum(onehots[k] * base, axis=-1, keepdims=True), lp)
    lp_ref[...] = lp.astype(I32)
    cnt_ref[...] = cnt


def _merge(x2, oa2, ob2, wg, wa, wb, wo, wr, br, ln_g, ln_b):
    t, d = x2.shape
    tm = ROW_TILE
    full = lambda shape: pl.BlockSpec(shape, lambda i: (0, 0))
    return pl.pallas_call(
        _merge_kernel,
        grid=(t // tm,),
        in_specs=[
            pl.BlockSpec((tm, d), lambda i: (i, 0)),
            pl.BlockSpec((tm, A_Q_W), lambda i: (i, 0)),
            pl.BlockSpec((tm, B_GROUP_W), lambda i: (i, 0)),
            full((d, 2 * d)), full((A_Q_W, d)), full((B_GROUP_W, d)), full((d, d)),
            full((d, ROUTER_PAD)), full((1, ROUTER_PAD)), full((1, d)), full((1, d)),
        ],
        out_specs=[
            pl.BlockSpec((tm, d), lambda i: (i, 0)),
            pl.BlockSpec((tm, ROUTER_PAD), lambda i: (i, 0)),
            pl.BlockSpec((tm, ROUTER_PAD), lambda i: (i, 0)),
            pl.BlockSpec((None, 1, ROUTER_PAD), lambda i: (i, 0, 0)),
        ],
        out_shape=[
            jax.ShapeDtypeStruct((t, d), F32),
            jax.ShapeDtypeStruct((t, ROUTER_PAD), I32),
            jax.ShapeDtypeStruct((t, ROUTER_PAD), F32),
            jax.ShapeDtypeStruct((t // tm, 1, ROUTER_PAD), F32),
        ],
        compiler_params=pltpu.CompilerParams(
            dimension_semantics=("arbitrary",), vmem_limit_bytes=VMEM_LIMIT),
        name="merge_ln_router",
    )(x2, oa2, ob2, wg, wa, wb, wo, wr, br, ln_g, ln_b)


def _pack_rows(v):
    vb = v.astype(BF16).astype(F32)
    lo = lax.bitcast_convert_type(vb[:, :HALF_W], U32) >> 16
    hi = lax.bitcast_convert_type(vb[:, HALF_W:], U32) & jnp.uint32(0xFFFF0000)
    return lo | hi


def _unpack_rows(w):
    lo = lax.bitcast_convert_type(w << 16, F32)
    hi = lax.bitcast_convert_type(w & jnp.uint32(0xFFFF0000), F32)
    return jnp.concatenate([lo, hi], axis=1).astype(BF16)


def _run_copy(tile, e, meta, vmem_rows, hbm_rows, sem, to_hbm):
    run_start_ref, n8_ref, off_ref = meta
    idx = tile * N_EXPERTS + e
    n = pl.multiple_of(n8_ref[idx], RUN_ALIGN)
    local = vmem_rows.at[pl.ds(pl.multiple_of(off_ref[idx], RUN_ALIGN), n)]
    remote = hbm_rows.at[pl.ds(pl.multiple_of(run_start_ref[idx], RUN_ALIGN), n)]
    return (pltpu.make_async_copy(local, remote, sem) if to_hbm
            else pltpu.make_async_copy(remote, local, sem)), n


def _start_runs(tile, meta, vmem_rows, hbm_rows, sem, to_hbm):
    def body(e, carry):
        copy, n = _run_copy(tile, e, meta, vmem_rows, hbm_rows, sem, to_hbm)

        @pl.when(n > 0)
        def _():
            copy.start()
        return carry
    lax.fori_loop(0, N_EXPERTS, body, 0)


def _wait_runs(tile, meta, vmem_rows, hbm_rows, sem, to_hbm):
    def body(e, carry):
        copy, n = _run_copy(tile, e, meta, vmem_rows, hbm_rows, sem, to_hbm)

        @pl.when(n > 0)
        def _():
            copy.wait()
        return carry
    lax.fori_loop(0, N_EXPERTS, body, 0)


def _dispatch_kernel(run_start_ref, n8_ref, off_ref, tail_start_ref, tail_len_ref, nused_ref,
                     h_ref, lpt_ref, x_hbm, buf, zbuf, sem, zsem):
    i = pl.program_id(0)
    nt = pl.num_programs(0)
    slot = i % 2
    meta = (run_start_ref, n8_ref, off_ref)

    @pl.when(i >= 2)
    def _():
        _wait_runs(i - 2, meta, buf.at[slot], x_hbm, sem.at[slot], True)

    hb = h_ref[...].astype(BF16)
    for c in range(PERM_ROWS // PERM_CHUNK):
        r0 = c * PERM_CHUNK
        rows = lax.broadcasted_iota(I32, (PERM_CHUNK, ROW_TILE), 0) + r0
        hit = rows == lpt_ref[0:1, :]
        for k in range(1, TOP_K):
            hit = hit | (rows == lpt_ref[k:k + 1, :])
        perm = jnp.where(hit, 1.0, 0.0).astype(BF16)
        buf[slot, r0:r0 + PERM_CHUNK, :] = _pack_rows(jnp.dot(perm, hb, preferred_element_type=F32))

    _start_runs(i, meta, buf.at[slot], x_hbm, sem.at[slot], True)

    @pl.when(i == nt - 1)
    def _():
        zbuf[...] = jnp.zeros_like(zbuf)

        def tail_copy(e):
            n = pl.multiple_of(tail_len_ref[e], RUN_ALIGN)
            dst = x_hbm.at[pl.ds(pl.multiple_of(tail_start_ref[e], RUN_ALIGN), n)]
            return pltpu.make_async_copy(zbuf.at[pl.ds(0, n)], dst, zsem), n

        def start_tail(e, carry):
            copy, n = tail_copy(e)

            @pl.when(n > 0)
            def _():
                copy.start()
            return carry

        def wait_tail(e, carry):
            copy, n = tail_copy(e)

            @pl.when(n > 0)
            def _():
                copy.wait()
            return carry

        lax.fori_loop(0, N_EXPERTS, start_tail, 0)
        lax.fori_loop(0, N_EXPERTS, wait_tail, 0)

        def spare_copy(blk_i):
            dst = x_hbm.at[pl.ds(pl.multiple_of(blk_i * MOE_BLOCK, MOE_BLOCK), MOE_BLOCK)]
            return pltpu.make_async_copy(zbuf, dst, zsem)

        def start_spare(blk_i, carry):
            spare_copy(blk_i).start()
            return carry

        def wait_spare(blk_i, carry):
            spare_copy(blk_i).wait()
            return carry

        n_blocks = x_hbm.shape[0] // MOE_BLOCK
        lax.fori_loop(nused_ref[0], n_blocks, start_spare, 0)
        lax.fori_loop(nused_ref[0], n_blocks, wait_spare, 0)

        @pl.when(nt >= 2)
        def _():
            _wait_runs(i - 1, meta, buf.at[1 - slot], x_hbm, sem.at[1 - slot], True)
        _wait_runs(i, meta, buf.at[slot], x_hbm, sem.at[slot], True)


def _dispatch(meta, tails, n_used, h2, lp_t, n_rows):
    t, d = h2.shape
    nt = t // ROW_TILE
    grid_spec = pltpu.PrefetchScalarGridSpec(
        num_scalar_prefetch=6,
        grid=(nt,),
        in_specs=[
            pl.BlockSpec((ROW_TILE, d), lambda i, *_: (i, 0)),
            pl.BlockSpec((None, TOP_K, ROW_TILE), lambda i, *_: (i, 0, 0)),
        ],
        out_specs=pl.BlockSpec(memory_space=pl.ANY),
        scratch_shapes=[
            pltpu.VMEM((2, PERM_ROWS, HALF_W), U32),
            pltpu.VMEM((MOE_BLOCK, HALF_W), U32),
            pltpu.SemaphoreType.DMA((2,)),
            pltpu.SemaphoreType.DMA(()),
        ],
    )
    return pl.pallas_call(
        _dispatch_kernel,
        grid_spec=grid_spec,
        out_shape=jax.ShapeDtypeStruct((n_rows, HALF_W), U32),
        compiler_params=pltpu.CompilerParams(
            dimension_semantics=("arbitrary",), vmem_limit_bytes=VMEM_LIMIT, has_side_effects=True),
        name="dispatch",
    )(*meta, *tails, n_used, h2, lp_t)


def _ffn_kernel(blk_e_ref, nused_ref, x_ref, wgu_ref, bgu_ref, wdn_ref, bdn_ref, y_ref):
    i = pl.program_id(0)

    @pl.when(i < nused_ref[0])
    def _():
        xb = _unpack_rows(x_ref[...])
        acc = jnp.zeros((MOE_BLOCK, D_MODEL), F32)
        for c in range(D_EXPERT // FFN_CHUNK):
            lo = c * FFN_CHUNK
            gate = jnp.dot(xb, wgu_ref[:, lo:lo + FFN_CHUNK], preferred_element_type=F32)
            gate = gate + bgu_ref[:, lo:lo + FFN_CHUNK]
            up = jnp.dot(xb, wgu_ref[:, D_EXPERT + lo:D_EXPERT + lo + FFN_CHUNK], preferred_element_type=F32)
            up = up + bgu_ref[:, D_EXPERT + lo:D_EXPERT + lo + FFN_CHUNK]
            gate = jnp.minimum(gate, SWIGLU_LIMIT)
            up = jnp.clip(up, -SWIGLU_LIMIT, SWIGLU_LIMIT)
            hmid = (up + 1.0) * gate * jax.nn.sigmoid(SWIGLU_ALPHA * gate)
            acc = acc + jnp.dot(hmid.astype(BF16), wdn_ref[lo:lo + FFN_CHUNK, :], preferred_element_type=F32)
        y_ref[...] = _pack_rows(acc + bdn_ref[...])

    @pl.when(i >= nused_ref[0])
    def _():
        y_ref[...] = jnp.zeros_like(y_ref)


def _expert_ffn(blk_e, n_used, x_rows, wgu, bgu, wdn, bdn):
    nblk = x_rows.shape[0] // MOE_BLOCK
    d = D_MODEL
    grid_spec = pltpu.PrefetchScalarGridSpec(
        num_scalar_prefetch=2,
        grid=(nblk,),
        in_specs=[
            pl.BlockSpec((MOE_BLOCK, HALF_W), lambda i, be, nu: (jnp.minimum(i, nu[0] - 1), 0)),
            pl.BlockSpec((None, d, 2 * D_EXPERT), lambda i, be, nu: (be[i], 0, 0)),
            pl.BlockSpec((None, 1, 2 * D_EXPERT), lambda i, be, nu: (be[i], 0, 0)),
            pl.BlockSpec((None, D_EXPERT, d), lambda i, be, nu: (be[i], 0, 0)),
            pl.BlockSpec((None, 1, d), lambda i, be, nu: (be[i], 0, 0)),
        ],
        out_specs=pl.BlockSpec((MOE_BLOCK, HALF_W), lambda i, be, nu: (i, 0)),
    )
    return pl.pallas_call(
        _ffn_kernel,
        grid_spec=grid_spec,
        out_shape=jax.ShapeDtypeStruct((nblk * MOE_BLOCK, HALF_W), U32),
        compiler_params=pltpu.CompilerParams(
            dimension_semantics=("arbitrary",), vmem_limit_bytes=VMEM_LIMIT),
        name="expert_ffn",
    )(blk_e, n_used, x_rows, wgu, bgu, wdn, bdn)


def _combine_kernel(run_start_ref, n8_ref, off_ref, y_hbm, h_ref, lp_ref, gw_ref, g_ref, b_ref,
                    out_ref, ybuf, sem):
    i = pl.program_id(0)
    nt = pl.num_programs(0)
    slot = i % 2
    meta = (run_start_ref, n8_ref, off_ref)

    @pl.when(i == 0)
    def _():
        ybuf[...] = jnp.zeros_like(ybuf)
        _start_runs(0, meta, ybuf.at[0], y_hbm, sem.at[0], False)

    @pl.when(i + 1 < nt)
    def _():
        _start_runs(i + 1, meta, ybuf.at[1 - slot], y_hbm, sem.at[1 - slot], False)

    _wait_runs(i, meta, ybuf.at[slot], y_hbm, sem.at[slot], False)

    lp = lp_ref[...]
    gw = gw_ref[...]
    ffn = jnp.zeros((ROW_TILE, D_MODEL), F32)
    for c in range(PERM_ROWS // PERM_CHUNK):
        r0 = c * PERM_CHUNK
        cols = lax.broadcasted_iota(I32, (ROW_TILE, PERM_CHUNK), 1) + r0
        wperm = jnp.zeros((ROW_TILE, PERM_CHUNK), F32)
        for k in range(TOP_K):
            wperm = jnp.where(cols == lp[:, k:k + 1], gw[:, k:k + 1], wperm)
        ysel = _unpack_rows(ybuf[slot, r0:r0 + PERM_CHUNK, :])
        ffn = ffn + jnp.dot(wperm.astype(BF16), ysel, preferred_element_type=F32)
    out_ref[...] = _layer_norm(DEEPNORM_ALPHA * h_ref[...] + ffn, g_ref[...], b_ref[...])


def _combine(meta, y_rows, h2, lp, gw, ln_g, ln_b):
    t, d = h2.shape
    nt = t // ROW_TILE
    grid_spec = pltpu.PrefetchScalarGridSpec(
        num_scalar_prefetch=3,
        grid=(nt,),
        in_specs=[
            pl.BlockSpec(memory_space=pl.ANY),
            pl.BlockSpec((ROW_TILE, d), lambda i, *_: (i, 0)),
            pl.BlockSpec((ROW_TILE, ROUTER_PAD), lambda i, *_: (i, 0)),
            pl.BlockSpec((ROW_TILE, ROUTER_PAD), lambda i, *_: (i, 0)),
            pl.BlockSpec((1, d), lambda i, *_: (0, 0)),
            pl.BlockSpec((1, d), lambda i, *_: (0, 0)),
        ],
        out_specs=pl.BlockSpec((ROW_TILE, d), lambda i, *_: (i, 0)),
        scratch_shapes=[pltpu.VMEM((2, PERM_ROWS, HALF_W), U32), pltpu.SemaphoreType.DMA((2,))],
    )
    return pl.pallas_call(
        _combine_kernel,
        grid_spec=grid_spec,
        out_shape=jax.ShapeDtypeStruct((t, d), F32),
        compiler_params=pltpu.CompilerParams(
            dimension_semantics=("arbitrary",), vmem_limit_bytes=VMEM_LIMIT),
        name="combine_ln",
    )(*meta, y_rows, h2, lp, gw, ln_g, ln_b)


def _max_blocks(t):
    nt = t // ROW_TILE
    rows = t * TOP_K + nt * N_EXPERTS * (RUN_ALIGN - 1) + N_EXPERTS * (MOE_BLOCK - RUN_ALIGN)
    return -(-rows // MOE_BLOCK)


def _layout(counts, nblk):
    n8 = (counts + RUN_ALIGN - 1) // RUN_ALIGN * RUN_ALIGN
    total = jnp.sum(n8, axis=0)
    padded = (total + MOE_BLOCK - 1) // MOE_BLOCK * MOE_BLOCK
    e_end = jnp.cumsum(padded)
    e_start = e_end - padded
    run_start = e_start[None, :] + jnp.cumsum(n8, axis=0) - n8
    off = jnp.cumsum(n8, axis=1) - n8
    blk_e = jnp.minimum(
        jnp.searchsorted(e_end, jnp.arange(nblk, dtype=I32) * MOE_BLOCK, side="right"), N_EXPERTS - 1)
    n_used = (e_end[-1] // MOE_BLOCK).reshape(1)
    meta = tuple(a.reshape(-1).astype(I32) for a in (run_start, n8, off))
    tails = ((e_start + total).astype(I32), (padded - total).astype(I32))
    return meta, tails, blk_e.astype(I32), n_used.astype(I32)


def kernel(x, w_in, attn_sinks, w_branch_a, w_branch_b, w_out, ln1_g, ln1_b, w_router, b_router,
           w_gate_up, b_gate_up, w_down, b_down, ln2_g, ln2_b):
    depth = w_in.shape[0]
    b, s, d = x.shape
    t = b * s
    nt = t // ROW_TILE

    pos = jnp.arange(s, dtype=F32)
    inv_freq = ROPE_THETA ** (-jnp.arange(0, HEAD_DIM, 2, dtype=F32) / HEAD_DIM)
    ang = pos[:, None] * inv_freq[None, :]
    cos32, sin32 = jnp.cos(ang), jnp.sin(ang)
    cos_t = jnp.tile(cos32, (1, LANES // (HEAD_DIM // 2)))
    sin_t = jnp.tile(jnp.concatenate([-sin32, sin32], axis=1), (1, LANES // HEAD_DIM))

    h = x
    for l in range(depth):
        wi = w_in[l]
        qscale = HEAD_DIM ** -0.5 * LOG2E
        cols = [wi[:, :A_Q_W] * qscale, wi[:, OFF_AK:OFF_BQ]]
        for g in range(len(B_GROUPS)):
            cols += [wi[:, OFF_BQ + g * B_GROUP_W:OFF_BQ + (g + 1) * B_GROUP_W] * qscale,
                     wi[:, OFF_BK + g * B_GROUP_W:OFF_BK + (g + 1) * B_GROUP_W],
                     wi[:, OFF_BV + g * B_GROUP_W:OFF_BV + (g + 1) * B_GROUP_W]]
        w_qkv = jnp.concatenate(cols, axis=1).astype(BF16)

        qkv_a, qkv_b0, qkv_b1, qkv_b2 = _qkv_rope(h, w_qkv, cos_t, sin_t)
        oa, ob = _attention(attn_sinks[l].astype(F32) * LOG2E, qkv_a, qkv_b0, qkv_b1, qkv_b2)

        wr = jnp.zeros((d, ROUTER_PAD), F32).at[:, :N_EXPERTS].set(w_router[l]).astype(BF16)
        br = jnp.full((1, ROUTER_PAD), NEG, F32).at[0, :N_EXPERTS].set(b_router[l].astype(F32))
        h1, lp, gw, counts = _merge(
            h.reshape(t, d), oa.reshape(t, A_Q_W), ob.reshape(t, B_GROUP_W),
            wi[:, OFF_G:].astype(BF16), w_branch_a[l].astype(BF16), w_branch_b[l].astype(BF16),
            w_out[l].astype(BF16), wr, br, ln1_g[l].reshape(1, d), ln1_b[l].reshape(1, d))

        nblk = _max_blocks(t)
        meta, tails, blk_e, n_used = _layout(counts.reshape(nt, ROUTER_PAD)[:, :N_EXPERTS].astype(I32), nblk)
        lp_t = lp[:, :TOP_K].reshape(nt, ROW_TILE, TOP_K).transpose(0, 2, 1)
        x_rows = _dispatch(meta, tails, n_used, h1, lp_t, nblk * MOE_BLOCK)
        y_rows = _expert_ffn(
            blk_e, n_used, x_rows,
            w_gate_up[l].astype(BF16), b_gate_up[l].reshape(N_EXPERTS, 1, 2 * D_EXPERT),
            w_down[l].astype(BF16), b_down[l].reshape(N_EXPERTS, 1, d))
        h = _combine(meta, y_rows, h1, lp, gw, ln2_g[l].reshape(1, d), ln2_b[l].reshape(1, d)).reshape(b, s, d)
    return h
```

```python
import jax
import jax.numpy as jnp
from jax import lax
from jax.experimental import pallas as pl
from jax.experimental.pallas import tpu as pltpu

F32 = jnp.float32
BF16 = jnp.bfloat16
I32 = jnp.int32
WORD = jnp.uint32

D_MODEL = 1024
HEAD_DIM = 64
ROPE_THETA = 10000.0
A_Q_W = 1024
A_KV_W = 256
A_KV_HEADS = 4
A_GQA = 4
A_WINDOW = 128
B_GROUPS = ((128, 1), (512, 4), (2048, 16))
B_GROUP_W = 256
B_HEADS = 4
B_W = 768
OFF_AK = A_Q_W
OFF_AV = OFF_AK + A_KV_W
OFF_BQ = OFF_AV + A_KV_W
OFF_BK = OFF_BQ + B_W
OFF_BV = OFF_BK + B_W
OFF_G = OFF_BV + B_W
N_EXPERTS = 32
TOP_K = 4
D_EXPERT = 1024
SWIGLU_LIMIT = 7.0
SWIGLU_ALPHA = 1.702
LN_EPS = 1e-5
DEEPNORM_ALPHA = 2.0 ** 0.25
LOG2E = 1.4426950408889634

LANES = 128
SUBLANES = 8
ATTN_BLOCK = 128
SEQ_TILE = 512
ROW_TILE = 512
MOE_BLOCK = 512
FFN_CHUNK = 512
RUN_ALIGN = SUBLANES
PERM_ROWS = ROW_TILE * TOP_K + N_EXPERTS * RUN_ALIGN
PERM_CHUNK = 256
A_SEG_W = A_Q_W + 2 * A_KV_W
B_SEG_W = 3 * B_GROUP_W
QKV_W = A_SEG_W + 3 * B_SEG_W
HALF_W = D_MODEL // 2
ROUTER_PAD = LANES
NEG = -1e30
VMEM_LIMIT = 48 * 1024 * 1024
FFN_VMEM_LIMIT = 56 * 1024 * 1024


def _qkv_rope_kernel(x_ref, w_ref, cos_ref, sin_ref, a_ref, b0_ref, b1_ref, b2_ref, scr_ref):
    tm = x_ref.shape[0]
    xb = x_ref[...].astype(BF16)
    cos = cos_ref[...]
    sin = sin_ref[...]
    lane = lax.broadcasted_iota(I32, (tm, LANES), 1)
    first_half = (lane % HEAD_DIM) < (HEAD_DIM // 2)

    def rope(a):
        partner = jnp.where(first_half, pltpu.roll(a, LANES - 32, 1), pltpu.roll(a, 32, 1))
        return a * cos + partner * sin

    def segment(col0, width, n_rope_chunks):
        acc = jnp.dot(xb, w_ref[:, col0:col0 + width], preferred_element_type=F32)
        pieces = []
        for c in range(width // LANES):
            a = acc[:, c * LANES:(c + 1) * LANES]
            pieces.append(rope(a) if c < n_rope_chunks else a)
        return pieces

    a_ref[...] = jnp.concatenate(segment(0, A_SEG_W, (A_Q_W + A_KV_W) // LANES), axis=1).astype(BF16)
    b0_ref[...] = jnp.concatenate(segment(A_SEG_W, B_SEG_W, 2 * B_GROUP_W // LANES), axis=1).astype(BF16)
    n_chunks = B_SEG_W // LANES
    for out_ref, col0, dil in ((b1_ref, A_SEG_W + B_SEG_W, 4), (b2_ref, A_SEG_W + 2 * B_SEG_W, 16)):
        for k, piece in enumerate(segment(col0, B_SEG_W, 2 * B_GROUP_W // LANES)):
            scr_ref[k] = piece
        for c in range(dil):
            rows = [scr_ref[k, pl.ds(c, tm // dil, stride=dil), :] for k in range(n_chunks)]
            out_ref[c] = jnp.concatenate(rows, axis=1).astype(BF16)


def _qkv_rope(x, w_qkv, cos_t, sin_t):
    b, s, d = x.shape
    tm = SEQ_TILE
    grid = (b, s // tm)
    return pl.pallas_call(
        _qkv_rope_kernel,
        grid=grid,
        in_specs=[
            pl.BlockSpec((None, tm, d), lambda bi, si: (bi, si, 0)),
            pl.BlockSpec((d, QKV_W), lambda bi, si: (0, 0)),
            pl.BlockSpec((tm, LANES), lambda bi, si: (si, 0)),
            pl.BlockSpec((tm, LANES), lambda bi, si: (si, 0)),
        ],
        out_specs=[
            pl.BlockSpec((None, tm, A_SEG_W), lambda bi, si: (bi, si, 0)),
            pl.BlockSpec((None, tm, B_SEG_W), lambda bi, si: (bi, si, 0)),
            pl.BlockSpec((None, 4, tm // 4, B_SEG_W), lambda bi, si: (bi, 0, si, 0)),
            pl.BlockSpec((None, 16, tm // 16, B_SEG_W), lambda bi, si: (bi, 0, si, 0)),
        ],
        out_shape=[
            jax.ShapeDtypeStruct((b, s, A_SEG_W), BF16),
            jax.ShapeDtypeStruct((b, s, B_SEG_W), BF16),
            jax.ShapeDtypeStruct((b, 4, s // 4, B_SEG_W), BF16),
            jax.ShapeDtypeStruct((b, 16, s // 16, B_SEG_W), BF16),
        ],
        scratch_shapes=[pltpu.VMEM((B_SEG_W // LANES, tm, LANES), F32)],
        compiler_params=pltpu.CompilerParams(
            dimension_semantics=("arbitrary", "arbitrary"), vmem_limit_bytes=VMEM_LIMIT),
        name="qkv_rope",
    )(x, w_qkv, cos_t, sin_t)


def _stacked_heads_unit(q, kmat, vmat, bias, sink):
    blk = q.shape[0]
    group = lax.broadcasted_iota(I32, (1, B_GROUP_W), 1) // HEAD_DIM
    zero = jnp.zeros_like(q)
    qs = jnp.concatenate([jnp.where(group == h, q, zero) for h in range(B_HEADS)], axis=0)
    s = lax.dot_general(qs, kmat, (((1,), (1,)), ((), ())), preferred_element_type=F32) + bias
    m = jnp.max(s, axis=-1, keepdims=True)
    if sink is not None:
        m = jnp.maximum(m, sink)
    p = jnp.exp2(s - m)
    l = jnp.sum(p, axis=-1, keepdims=True)
    if sink is not None:
        l = l + jnp.exp2(sink - m)
    o = jnp.dot(p.astype(BF16), vmat, preferred_element_type=F32)
    scale = 1.0 / l
    lse = m + jnp.log2(l)
    out = jnp.zeros((blk, B_GROUP_W), F32)
    scale_b = jnp.zeros((blk, B_GROUP_W), F32)
    lse_b = jnp.zeros((blk, B_GROUP_W), F32)
    for h in range(B_HEADS):
        out = jnp.where(group == h, o[h * blk:(h + 1) * blk], out)
        scale_b = jnp.where(group == h, scale[h * blk:(h + 1) * blk], scale_b)
        lse_b = jnp.where(group == h, lse[h * blk:(h + 1) * blk], lse_b)
    return out * scale_b, lse_b


def _store_lane_chunks(ref, lead, rows, x):
    for c in range(x.shape[1] // LANES):
        ref[(*lead, c, rows, slice(None))] = x[:, c * LANES:(c + 1) * LANES]


def _attention_kernel(sink_ref, a_cur, a_prev, b0_cur, b0_prev, b1_cur, b1_prev, b2_ref,
                      oa_ref, ob_ref, kexp_scr, vexp_scr, kvb_scr, o_scr, lse_scr, o2_scr, lse2_scr,
                      band_scr, causal_scr):
    j = pl.program_id(1)
    blk = ATTN_BLOCK
    nqb = SEQ_TILE // blk
    stack = B_HEADS * blk

    @pl.when(jnp.logical_and(pl.program_id(0) == 0, j == 0))
    def _():
        row = lax.broadcasted_iota(I32, (stack, 2 * blk), 0) % blk
        col = lax.broadcasted_iota(I32, (stack, 2 * blk), 1)
        dist = blk + row - col
        for w, n_back in enumerate((A_WINDOW - 1, B_GROUPS[0][0] // B_GROUPS[0][1])):
            in_band = (dist >= 0) & (dist <= n_back)
            band_scr[2 * w] = jnp.where(in_band, 0.0, NEG)
            band_scr[2 * w + 1] = jnp.where(in_band & (col >= blk), 0.0, NEG)
        r1 = lax.broadcasted_iota(I32, (stack, blk), 0) % blk
        c1 = lax.broadcasted_iota(I32, (stack, blk), 1)
        causal_scr[...] = jnp.where(c1 <= r1, 0.0, NEG)

    @pl.when(j == 0)
    def _():
        def class_body(c, carry):
            t = b2_ref[c]
            out, lse_b = _stacked_heads_unit(t[:, :B_GROUP_W], t[:, B_GROUP_W:2 * B_GROUP_W],
                                             t[:, 2 * B_GROUP_W:], causal_scr[...], None)
            rows = pl.ds(c, blk, stride=16)
            _store_lane_chunks(o2_scr, (), rows, out)
            _store_lane_chunks(lse2_scr, (), rows, lse_b)
            return carry

        lax.fori_loop(0, 16, class_body, 0, unroll=4)

    for src, r0, r1 in ((a_prev, 0, blk), (a_cur, blk, blk + SEQ_TILE)):
        for dst, col0 in ((kexp_scr, A_Q_W), (vexp_scr, A_Q_W + A_KV_W)):
            heads = src[:, col0:col0 + A_KV_W]
            for kvh in range(A_KV_HEADS):
                head = heads[:, kvh * HEAD_DIM:(kvh + 1) * HEAD_DIM]
                dst[r0:r1, kvh * B_GROUP_W:(kvh + 1) * B_GROUP_W] = jnp.concatenate([head] * A_GQA, axis=1)
    kvb_scr[0:blk, :] = b0_prev[:, B_GROUP_W:]
    kvb_scr[blk:, :] = b0_cur[:, B_GROUP_W:]

    row_block = lax.broadcasted_iota(I32, (stack, 1), 0) // blk

    def qblock_body(qb, carry):
        r0 = pl.multiple_of(qb * blk, blk)
        first = jnp.logical_and(j == 0, qb == 0).astype(I32)
        for kvh in range(A_KV_HEADS):
            lanes = slice(kvh * B_GROUP_W, (kvh + 1) * B_GROUP_W)
            sink = jnp.zeros((stack, 1), F32)
            for g in range(A_GQA):
                sink = jnp.where(row_block == g, sink_ref[kvh * A_GQA + g], sink)
            out, _ = _stacked_heads_unit(a_cur[pl.ds(r0, blk), lanes], kexp_scr[pl.ds(r0, 2 * blk), lanes],
                                         vexp_scr[pl.ds(r0, 2 * blk), lanes], band_scr[first], sink)
            oa_ref[pl.ds(r0, blk), lanes] = out.astype(oa_ref.dtype)
        out, lse_b = _stacked_heads_unit(b0_cur[pl.ds(r0, blk), 0:B_GROUP_W],
                                         kvb_scr[pl.ds(r0, 2 * blk), 0:B_GROUP_W],
                                         kvb_scr[pl.ds(r0, 2 * blk), B_GROUP_W:], band_scr[2 + first], None)
        _store_lane_chunks(o_scr, (0,), pl.ds(r0, blk), out)
        _store_lane_chunks(lse_scr, (0,), pl.ds(r0, blk), lse_b)
        return carry

    lax.fori_loop(0, nqb, qblock_body, 0, unroll=True)

    first_class_block = (j == 0).astype(I32)

    def class4_body(c, carry):
        cur = b1_cur[c]
        prev = b1_prev[c]
        kv = jnp.concatenate([prev[:, B_GROUP_W:], cur[:, B_GROUP_W:]], axis=0)
        out, lse_b = _stacked_heads_unit(cur[:, :B_GROUP_W], kv[:, :B_GROUP_W], kv[:, B_GROUP_W:],
                                         band_scr[2 + first_class_block], None)
        rows = pl.ds(c, blk, stride=4)
        _store_lane_chunks(o_scr, (1,), rows, out)
        _store_lane_chunks(lse_scr, (1,), rows, lse_b)
        return carry

    lax.fori_loop(0, 4, class4_body, 0, unroll=True)

    t0 = pl.multiple_of(j * SEQ_TILE, SEQ_TILE)
    for c in range(B_GROUP_W // LANES):
        l0, l1, l2 = lse_scr[0, c], lse_scr[1, c], lse2_scr[c, pl.ds(t0, SEQ_TILE), :]
        mx = jnp.maximum(jnp.maximum(l0, l1), l2)
        w0, w1, w2 = jnp.exp2(l0 - mx), jnp.exp2(l1 - mx), jnp.exp2(l2 - mx)
        num = w0 * o_scr[0, c] + w1 * o_scr[1, c] + w2 * o2_scr[c, pl.ds(t0, SEQ_TILE), :]
        ob_ref[:, c * LANES:(c + 1) * LANES] = (num / (w0 + w1 + w2)).astype(ob_ref.dtype)


def _attention(sinks, qkv_a, qkv_b0, qkv_b1, qkv_b2):
    b, s, _ = qkv_a.shape
    t = SEQ_TILE
    blk = ATTN_BLOCK
    nq = t // blk
    grid = (b, s // t)
    n_chunks = B_GROUP_W // LANES
    return pl.pallas_call(
        _attention_kernel,
        grid=grid,
        in_specs=[
            pl.BlockSpec(memory_space=pltpu.SMEM),
            pl.BlockSpec((None, t, A_SEG_W), lambda bi, j: (bi, j, 0)),
            pl.BlockSpec((None, blk, A_SEG_W), lambda bi, j: (bi, jnp.maximum(j * nq - 1, 0), 0)),
            pl.BlockSpec((None, t, B_SEG_W), lambda bi, j: (bi, j, 0)),
            pl.BlockSpec((None, blk, B_SEG_W), lambda bi, j: (bi, jnp.maximum(j * nq - 1, 0), 0)),
            pl.BlockSpec((None, 4, blk, B_SEG_W), lambda bi, j: (bi, 0, j, 0)),
            pl.BlockSpec((None, 4, blk, B_SEG_W), lambda bi, j: (bi, 0, jnp.maximum(j - 1, 0), 0)),
            pl.BlockSpec((None, 16, blk, B_SEG_W), lambda bi, j: (bi, 0, 0, 0)),
        ],
        out_specs=[
            pl.BlockSpec((None, t, A_Q_W), lambda bi, j: (bi, j, 0)),
            pl.BlockSpec((None, t, B_GROUP_W), lambda bi, j: (bi, j, 0)),
        ],
        out_shape=[
            jax.ShapeDtypeStruct((b, s, A_Q_W), BF16),
            jax.ShapeDtypeStruct((b, s, B_GROUP_W), BF16),
        ],
        scratch_shapes=[
            pltpu.VMEM((blk + t, A_KV_HEADS * B_GROUP_W), BF16),
            pltpu.VMEM((blk + t, A_KV_HEADS * B_GROUP_W), BF16),
            pltpu.VMEM((blk + t, 2 * B_GROUP_W), BF16),
            pltpu.VMEM((2, n_chunks, t, LANES), F32),
            pltpu.VMEM((2, n_chunks, t, LANES), F32),
            pltpu.VMEM((n_chunks, s, LANES), F32),
            pltpu.VMEM((n_chunks, s, LANES), F32),
            pltpu.VMEM((4, B_HEADS * blk, 2 * blk), F32),
            pltpu.VMEM((B_HEADS * blk, blk), F32),
        ],
        compiler_params=pltpu.CompilerParams(
            dimension_semantics=("arbitrary", "arbitrary"), vmem_limit_bytes=VMEM_LIMIT),
        name="banded_attention",
    )(sinks, qkv_a, qkv_a, qkv_b0, qkv_b0, qkv_b1, qkv_b1, qkv_b2)


def _layer_norm(z, g, b):
    mu = jnp.mean(z, axis=-1, keepdims=True)
    zc = z - mu
    var = jnp.mean(zc * zc, axis=-1, keepdims=True)
    return zc * lax.rsqrt(var + LN_EPS) * g + b


def _merge_kernel(x_ref, oa_ref, ob_ref, wg_ref, wa_ref, wb_ref, wo_ref, wr_ref, br_ref, g_ref, b_ref,
                  h_ref, lp_ref, gw_ref, cnt_ref, logits_scr):
    i = pl.program_id(0)

    @pl.when(i == 0)
    def _():
        logits_scr[...] = jnp.zeros_like(logits_scr)

    logits = logits_scr[(i + 1) % 2]

    x = x_ref[...]
    xb = x.astype(BF16)
    gates = jax.nn.sigmoid(jnp.dot(xb, wg_ref[...], preferred_element_type=F32))
    ya = jnp.dot(oa_ref[...], wa_ref[...], preferred_element_type=F32)
    yb = jnp.dot(ob_ref[...], wb_ref[...], preferred_element_type=F32)
    merged = gates[:, :D_MODEL] * ya + gates[:, D_MODEL:] * yb
    mix = jnp.dot(merged.astype(BF16), wo_ref[...], preferred_element_type=F32)
    h = _layer_norm(DEEPNORM_ALPHA * x + mix, g_ref[...], b_ref[...])
    h_ref[...] = h
    logits_scr[i % 2] = jnp.dot(h.astype(BF16), wr_ref[...], preferred_element_type=F32) + br_ref[...]

    tm = logits.shape[0]
    lane = lax.broadcasted_iota(I32, (tm, ROUTER_PAD), 1)
    lane_f = lane.astype(F32)
    v_out = jnp.full((tm, ROUTER_PAD), NEG, F32)
    onehots = []
    lg = logits
    for k in range(TOP_K):
        m = jnp.max(lg, axis=-1, keepdims=True)
        idx = jnp.min(jnp.where(lg == m, lane_f, float(ROUTER_PAD)), axis=-1, keepdims=True)
        chosen = lane_f == idx
        onehots.append(chosen.astype(F32))
        v_out = jnp.where(lane == k, m, v_out)
        lg = jnp.where(chosen, 3.0 * NEG, lg)
    vmax = jnp.max(v_out, axis=-1, keepdims=True)
    ex = jnp.exp(v_out - vmax)
    gw_ref[...] = ex / jnp.sum(ex, axis=-1, keepdims=True)

    routed = onehots[0] + onehots[1] + onehots[2] + onehots[3]
    r_i = lax.broadcasted_iota(I32, (tm, tm), 0)
    c_i = lax.broadcasted_iota(I32, (tm, tm), 1)
    before = jnp.dot((c_i < r_i).astype(BF16), routed.astype(BF16), preferred_element_type=F32)
    cnt = jnp.sum(routed, axis=0, keepdims=True)
    runs = jnp.ceil(cnt * (1.0 / RUN_ALIGN))
    e_r = lax.broadcasted_iota(I32, (ROUTER_PAD, ROUTER_PAD), 0)
    e_c = lax.broadcasted_iota(I32, (ROUTER_PAD, ROUTER_PAD), 1)
    off = jnp.dot(jnp.broadcast_to(runs, (SUBLANES, ROUTER_PAD)).astype(BF16), (e_r < e_c).astype(BF16),
                  preferred_element_type=F32)[0:1] * float(RUN_ALIGN)
    base = before + off
    lp = jnp.zeros((tm, ROUTER_PAD), F32)
    for k in range(TOP_K):
        lp = jnp.where(lane == k, jnp.sum(onehots[k] * base, axis=-1, keepdims=True), lp)
    lp_ref[...] = lp.astype(I32)
    cnt_ref[...] = cnt


def _merge(x2, oa2, ob2, wg, wa, wb, wo, wr, br, ln_g, ln_b):
    t, d = x2.shape
    tm = ROW_TILE
    nt = t // tm
    full = lambda shape: pl.BlockSpec(shape, lambda i: (0, 0))
    dense = lambda i: (jnp.minimum(i, nt - 1), 0)
    routed = lambda i: (jnp.maximum(i - 1, 0), 0)
    return pl.pallas_call(
        _merge_kernel,
        grid=(nt + 1,),
        in_specs=[
            pl.BlockSpec((tm, d), dense),
            pl.BlockSpec((tm, A_Q_W), dense),
            pl.BlockSpec((tm, B_GROUP_W), dense),
            full((d, 2 * d)), full((A_Q_W, d)), full((B_GROUP_W, d)), full((d, d)),
            full((d, ROUTER_PAD)), full((1, ROUTER_PAD)), full((1, d)), full((1, d)),
        ],
        out_specs=[
            pl.BlockSpec((tm, d), dense),
            pl.BlockSpec((tm, ROUTER_PAD), routed),
            pl.BlockSpec((tm, ROUTER_PAD), routed),
            pl.BlockSpec((None, 1, ROUTER_PAD), lambda i: (jnp.maximum(i - 1, 0), 0, 0)),
        ],
        out_shape=[
            jax.ShapeDtypeStruct((t, d), F32),
            jax.ShapeDtypeStruct((t, ROUTER_PAD), I32),
            jax.ShapeDtypeStruct((t, ROUTER_PAD), F32),
            jax.ShapeDtypeStruct((t // tm, 1, ROUTER_PAD), F32),
        ],
        scratch_shapes=[pltpu.VMEM((2, tm, ROUTER_PAD), F32)],
        compiler_params=pltpu.CompilerParams(
            dimension_semantics=("arbitrary",), vmem_limit_bytes=VMEM_LIMIT),
        name="merge_ln_router",
    )(x2, oa2, ob2, wg, wa, wb, wo, wr, br, ln_g, ln_b)


def _pack_rows(v, already_bf16=False):
    vb = v if already_bf16 else v.astype(BF16).astype(F32)
    lo = lax.bitcast_convert_type(vb[:, :HALF_W], WORD) >> 16
    hi = lax.bitcast_convert_type(vb[:, HALF_W:], WORD) & jnp.uint32(0xFFFF0000)
    return lo | hi


def _unpack_rows(w):
    lo = lax.bitcast_convert_type(w << 16, F32)
    hi = lax.bitcast_convert_type(w & jnp.uint32(0xFFFF0000), F32)
    return jnp.concatenate([lo, hi], axis=1).astype(BF16)


def _run_copy(tile, e, meta, vmem_rows, hbm_rows, sem, to_hbm):
    run_start_ref, n8_ref, off_ref = meta
    idx = tile * N_EXPERTS + e
    n = pl.multiple_of(n8_ref[idx], RUN_ALIGN)
    local = vmem_rows.at[pl.ds(pl.multiple_of(off_ref[idx], RUN_ALIGN), n)]
    remote = hbm_rows.at[pl.ds(pl.multiple_of(run_start_ref[idx], RUN_ALIGN), n)]
    return (pltpu.make_async_copy(local, remote, sem) if to_hbm
            else pltpu.make_async_copy(remote, local, sem)), n


def _start_runs(tile, meta, vmem_rows, hbm_rows, sem, to_hbm):
    def body(e, carry):
        copy, n = _run_copy(tile, e, meta, vmem_rows, hbm_rows, sem, to_hbm)

        @pl.when(n > 0)
        def _():
            copy.start()
        return carry
    lax.fori_loop(0, N_EXPERTS, body, 0)


def _wait_runs(tile, meta, vmem_rows, hbm_rows, sem, to_hbm):
    def body(e, carry):
        copy, n = _run_copy(tile, e, meta, vmem_rows, hbm_rows, sem, to_hbm)

        @pl.when(n > 0)
        def _():
            copy.wait()
        return carry
    lax.fori_loop(0, N_EXPERTS, body, 0)


def _dispatch_kernel(run_start_ref, n8_ref, off_ref, tail_start_ref, tail_len_ref, nused_ref,
                     h_ref, lpt_ref, x_hbm, buf, zbuf, sem, zsem):
    i = pl.program_id(0)
    nt = pl.num_programs(0)
    slot = i % 2
    meta = (run_start_ref, n8_ref, off_ref)

    @pl.when(i >= 2)
    def _():
        _wait_runs(i - 2, meta, buf.at[slot], x_hbm, sem.at[slot], True)

    hb = h_ref[...].astype(BF16)
    lp16 = lpt_ref[...].astype(jnp.int16)
    one = jnp.ones((PERM_CHUNK, ROW_TILE), BF16)
    for c in range(PERM_ROWS // PERM_CHUNK):
        r0 = c * PERM_CHUNK
        rows = (lax.broadcasted_iota(I32, (PERM_CHUNK, ROW_TILE), 0) + r0).astype(jnp.int16)
        perm = jnp.zeros((PERM_CHUNK, ROW_TILE), BF16)
        for k in range(TOP_K):
            perm = jnp.where(rows == lp16[k:k + 1, :], one, perm)
        buf[slot, r0:r0 + PERM_CHUNK, :] = _pack_rows(jnp.dot(perm, hb, preferred_element_type=F32),
                                                      already_bf16=True)

    _start_runs(i, meta, buf.at[slot], x_hbm, sem.at[slot], True)

    @pl.when(i == nt - 1)
    def _():
        zbuf[...] = jnp.zeros_like(zbuf)

        def tail_copy(e):
            n = pl.multiple_of(tail_len_ref[e], RUN_ALIGN)
            dst = x_hbm.at[pl.ds(pl.multiple_of(tail_start_ref[e], RUN_ALIGN), n)]
            return pltpu.make_async_copy(zbuf.at[pl.ds(0, n)], dst, zsem), n

        def start_tail(e, carry):
            copy, n = tail_copy(e)

            @pl.when(n > 0)
            def _():
                copy.start()
            return carry

        def wait_tail(e, carry):
            copy, n = tail_copy(e)

            @pl.when(n > 0)
            def _():
                copy.wait()
            return carry

        lax.fori_loop(0, N_EXPERTS, start_tail, 0)
        lax.fori_loop(0, N_EXPERTS, wait_tail, 0)

        def spare_copy(blk_i):
            dst = x_hbm.at[pl.ds(pl.multiple_of(blk_i * MOE_BLOCK, MOE_BLOCK), MOE_BLOCK)]
            return pltpu.make_async_copy(zbuf, dst, zsem)

        def start_spare(blk_i, carry):
            spare_copy(blk_i).start()
            return carry

        def wait_spare(blk_i, carry):
            spare_copy(blk_i).wait()
            return carry

        n_blocks = x_hbm.shape[0] // MOE_BLOCK
        lax.fori_loop(nused_ref[0], n_blocks, start_spare, 0)
        lax.fori_loop(nused_ref[0], n_blocks, wait_spare, 0)

        @pl.when(nt >= 2)
        def _():
            _wait_runs(i - 1, meta, buf.at[1 - slot], x_hbm, sem.at[1 - slot], True)
        _wait_runs(i, meta, buf.at[slot], x_hbm, sem.at[slot], True)


def _dispatch(meta, tails, n_used, h2, lp_t, n_rows):
    t, d = h2.shape
    nt = t // ROW_TILE
    grid_spec = pltpu.PrefetchScalarGridSpec(
        num_scalar_prefetch=6,
        grid=(nt,),
        in_specs=[
            pl.BlockSpec((ROW_TILE, d), lambda i, *_: (i, 0)),
            pl.BlockSpec((None, TOP_K, ROW_TILE), lambda i, *_: (i, 0, 0)),
        ],
        out_specs=pl.BlockSpec(memory_space=pl.ANY),
        scratch_shapes=[
            pltpu.VMEM((2, PERM_ROWS, HALF_W), WORD),
            pltpu.VMEM((MOE_BLOCK, HALF_W), WORD),
            pltpu.SemaphoreType.DMA((2,)),
            pltpu.SemaphoreType.DMA(()),
        ],
    )
    return pl.pallas_call(
        _dispatch_kernel,
        grid_spec=grid_spec,
        out_shape=jax.ShapeDtypeStruct((n_rows, HALF_W), WORD),
        compiler_params=pltpu.CompilerParams(
            dimension_semantics=("arbitrary",), vmem_limit_bytes=VMEM_LIMIT, has_side_effects=True),
        name="dispatch",
    )(*meta, *tails, n_used, h2, lp_t)


def _ffn_kernel(blk_e_ref, nused_ref, x_ref, wgu_ref, bgu_ref, wdn_ref, bdn_ref, y_ref, wgu_bf, wdn_bf):
    i = pl.program_id(0)
    used = i < nused_ref[0]
    new_expert = jnp.logical_or(i == 0, blk_e_ref[i] != blk_e_ref[jnp.maximum(i - 1, 0)])

    @pl.when(jnp.logical_and(used, new_expert))
    def _():
        wgu_bf[...] = wgu_ref[...].astype(BF16)
        wdn_bf[...] = wdn_ref[...].astype(BF16)

    @pl.when(used)
    def _():
        xb = _unpack_rows(x_ref[...])
        acc = jnp.zeros((MOE_BLOCK, D_MODEL), F32)
        for c in range(D_EXPERT // FFN_CHUNK):
            lo = c * FFN_CHUNK
            gate = jnp.dot(xb, wgu_bf[:, lo:lo + FFN_CHUNK], preferred_element_type=F32)
            gate = gate + bgu_ref[:, lo:lo + FFN_CHUNK]
            up = jnp.dot(xb, wgu_bf[:, D_EXPERT + lo:D_EXPERT + lo + FFN_CHUNK], preferred_element_type=F32)
            up = up + bgu_ref[:, D_EXPERT + lo:D_EXPERT + lo + FFN_CHUNK]
            gate = jnp.minimum(gate, SWIGLU_LIMIT)
            up = jnp.clip(up, -SWIGLU_LIMIT, SWIGLU_LIMIT)
            hmid = (up + 1.0) * gate * jax.nn.sigmoid(SWIGLU_ALPHA * gate)
            acc = acc + jnp.dot(hmid.astype(BF16), wdn_bf[lo:lo + FFN_CHUNK, :], preferred_element_type=F32)
        y_ref[...] = _pack_rows(acc + bdn_ref[...])

    @pl.when(i >= nused_ref[0])
    def _():
        y_ref[...] = jnp.zeros_like(y_ref)


def _expert_ffn(blk_e, n_used, x_rows, wgu, bgu, wdn, bdn):
    nblk = x_rows.shape[0] // MOE_BLOCK
    d = D_MODEL
    grid_spec = pltpu.PrefetchScalarGridSpec(
        num_scalar_prefetch=2,
        grid=(nblk,),
        in_specs=[
            pl.BlockSpec((MOE_BLOCK, HALF_W), lambda i, be, nu: (jnp.minimum(i, nu[0] - 1), 0)),
            pl.BlockSpec((None, d, 2 * D_EXPERT), lambda i, be, nu: (be[i], 0, 0)),
            pl.BlockSpec((None, 1, 2 * D_EXPERT), lambda i, be, nu: (be[i], 0, 0)),
            pl.BlockSpec((None, D_EXPERT, d), lambda i, be, nu: (be[i], 0, 0)),
            pl.BlockSpec((None, 1, d), lambda i, be, nu: (be[i], 0, 0)),
        ],
        out_specs=pl.BlockSpec((MOE_BLOCK, HALF_W), lambda i, be, nu: (i, 0)),
        scratch_shapes=[pltpu.VMEM((d, 2 * D_EXPERT), BF16), pltpu.VMEM((D_EXPERT, d), BF16)],
    )
    return pl.pallas_call(
        _ffn_kernel,
        grid_spec=grid_spec,
        out_shape=jax.ShapeDtypeStruct((nblk * MOE_BLOCK, HALF_W), WORD),
        compiler_params=pltpu.CompilerParams(
            dimension_semantics=("arbitrary",), vmem_limit_bytes=FFN_VMEM_LIMIT),
        name="expert_ffn",
    )(blk_e, n_used, x_rows, wgu, bgu, wdn, bdn)


def _combine_kernel(run_start_ref, n8_ref, off_ref, y_hbm, h_ref, lpt_ref, gwt_ref, g_ref, b_ref,
                    out_ref, ybuf, wperm_scr, ysel_scr, sem):
    i = pl.program_id(0)
    nt = pl.num_programs(0)
    slot = i % 2
    meta = (run_start_ref, n8_ref, off_ref)

    @pl.when(i == 0)
    def _():
        ybuf[...] = jnp.zeros_like(ybuf)
        _start_runs(0, meta, ybuf.at[0], y_hbm, sem.at[0], False)

    @pl.when(i + 1 < nt)
    def _():
        _start_runs(i + 1, meta, ybuf.at[1 - slot], y_hbm, sem.at[1 - slot], False)

    _wait_runs(i, meta, ybuf.at[slot], y_hbm, sem.at[slot], False)

    lp16 = lpt_ref[...].astype(jnp.int16)
    gw16 = gwt_ref[...].astype(BF16)
    for c in range(PERM_ROWS // PERM_CHUNK):
        r0 = c * PERM_CHUNK
        rows = (lax.broadcasted_iota(I32, (PERM_CHUNK, ROW_TILE), 0) + r0).astype(jnp.int16)
        wperm = jnp.zeros((PERM_CHUNK, ROW_TILE), BF16)
        for k in range(TOP_K):
            wperm = jnp.where(rows == lp16[k:k + 1, :], gw16[k:k + 1, :], wperm)
        wperm_scr[r0:r0 + PERM_CHUNK, :] = wperm
        ysel_scr[r0:r0 + PERM_CHUNK, :] = _unpack_rows(ybuf[slot, r0:r0 + PERM_CHUNK, :])
    ffn = lax.dot_general(wperm_scr[...], ysel_scr[...], (((0,), (0,)), ((), ())),
                          preferred_element_type=F32)
    out_ref[...] = _layer_norm(DEEPNORM_ALPHA * h_ref[...] + ffn, g_ref[...], b_ref[...])


def _combine(meta, y_rows, h2, lp_t, gw_t, ln_g, ln_b):
    t, d = h2.shape
    nt = t // ROW_TILE
    grid_spec = pltpu.PrefetchScalarGridSpec(
        num_scalar_prefetch=3,
        grid=(nt,),
        in_specs=[
            pl.BlockSpec(memory_space=pl.ANY),
            pl.BlockSpec((ROW_TILE, d), lambda i, *_: (i, 0)),
            pl.BlockSpec((None, TOP_K, ROW_TILE), lambda i, *_: (i, 0, 0)),
            pl.BlockSpec((None, TOP_K, ROW_TILE), lambda i, *_: (i, 0, 0)),
            pl.BlockSpec((1, d), lambda i, *_: (0, 0)),
            pl.BlockSpec((1, d), lambda i, *_: (0, 0)),
        ],
        out_specs=pl.BlockSpec((ROW_TILE, d), lambda i, *_: (i, 0)),
        scratch_shapes=[pltpu.VMEM((2, PERM_ROWS, HALF_W), WORD), pltpu.VMEM((PERM_ROWS, ROW_TILE), BF16),
                        pltpu.VMEM((PERM_ROWS, d), BF16), pltpu.SemaphoreType.DMA((2,))],
    )
    return pl.pallas_call(
        _combine_kernel,
        grid_spec=grid_spec,
        out_shape=jax.ShapeDtypeStruct((t, d), F32),
        compiler_params=pltpu.CompilerParams(
            dimension_semantics=("arbitrary",), vmem_limit_bytes=VMEM_LIMIT),
        name="combine_ln",
    )(*meta, y_rows, h2, lp_t, gw_t, ln_g, ln_b)


def _max_blocks(t):
    nt = t // ROW_TILE
    rows = t * TOP_K + nt * N_EXPERTS * (RUN_ALIGN - 1) + N_EXPERTS * (MOE_BLOCK - RUN_ALIGN)
    return -(-rows // MOE_BLOCK)


def _layout(counts, nblk):
    n8 = (counts + RUN_ALIGN - 1) // RUN_ALIGN * RUN_ALIGN
    total = jnp.sum(n8, axis=0)
    padded = (total + MOE_BLOCK - 1) // MOE_BLOCK * MOE_BLOCK
    e_end = jnp.cumsum(padded)
    e_start = e_end - padded
    run_start = e_start[None, :] + jnp.cumsum(n8, axis=0) - n8
    off = jnp.cumsum(n8, axis=1) - n8
    blk_row0 = jnp.arange(nblk, dtype=I32) * MOE_BLOCK
    blk_e = jnp.minimum(jnp.sum((e_end[None, :] <= blk_row0[:, None]).astype(I32), axis=1), N_EXPERTS - 1)
    n_used = (e_end[-1] // MOE_BLOCK).reshape(1)
    meta = tuple(a.reshape(-1).astype(I32) for a in (run_start, n8, off))
    tails = ((e_start + total).astype(I32), (padded - total).astype(I32))
    return meta, tails, blk_e.astype(I32), n_used.astype(I32)


def kernel(x, w_in, attn_sinks, w_branch_a, w_branch_b, w_out, ln1_g, ln1_b, w_router, b_router,
           w_gate_up, b_gate_up, w_down, b_down, ln2_g, ln2_b):
    depth = w_in.shape[0]
    b, s, d = x.shape
    t = b * s
    nt = t // ROW_TILE

    pos = jnp.arange(s, dtype=F32)
    inv_freq = ROPE_THETA ** (-jnp.arange(0, HEAD_DIM, 2, dtype=F32) / HEAD_DIM)
    ang = pos[:, None] * inv_freq[None, :]
    cos32, sin32 = jnp.cos(ang), jnp.sin(ang)
    cos_t = jnp.tile(cos32, (1, LANES // (HEAD_DIM // 2)))
    sin_t = jnp.tile(jnp.concatenate([-sin32, sin32], axis=1), (1, LANES // HEAD_DIM))

    h = x
    for l in range(depth):
        wi = w_in[l]
        qscale = HEAD_DIM ** -0.5 * LOG2E
        cols = [wi[:, :A_Q_W] * qscale, wi[:, OFF_AK:OFF_BQ]]
        for g in range(len(B_GROUPS)):
            cols += [wi[:, OFF_BQ + g * B_GROUP_W:OFF_BQ + (g + 1) * B_GROUP_W] * qscale,
                     wi[:, OFF_BK + g * B_GROUP_W:OFF_BK + (g + 1) * B_GROUP_W],
                     wi[:, OFF_BV + g * B_GROUP_W:OFF_BV + (g + 1) * B_GROUP_W]]
        w_qkv = jnp.concatenate(cols, axis=1).astype(BF16)

        qkv_a, qkv_b0, qkv_b1, qkv_b2 = _qkv_rope(h, w_qkv, cos_t, sin_t)
        oa, ob = _attention(attn_sinks[l].astype(F32) * LOG2E, qkv_a, qkv_b0, qkv_b1, qkv_b2)

        wr = jnp.zeros((d, ROUTER_PAD), F32).at[:, :N_EXPERTS].set(w_router[l]).astype(BF16)
        br = jnp.full((1, ROUTER_PAD), NEG, F32).at[0, :N_EXPERTS].set(b_router[l].astype(F32))
        h1, lp, gw, counts = _merge(
            h.reshape(t, d), oa.reshape(t, A_Q_W), ob.reshape(t, B_GROUP_W),
            wi[:, OFF_G:].astype(BF16), w_branch_a[l].astype(BF16), w_branch_b[l].astype(BF16),
            w_out[l].astype(BF16), wr, br, ln1_g[l].reshape(1, d), ln1_b[l].reshape(1, d))

        nblk = _max_blocks(t)
        meta, tails, blk_e, n_used = _layout(counts.reshape(nt, ROUTER_PAD)[:, :N_EXPERTS].astype(I32), nblk)
        lp_t = lp[:, :TOP_K].reshape(nt, ROW_TILE, TOP_K).transpose(0, 2, 1)
        gw_t = gw[:, :TOP_K].reshape(nt, ROW_TILE, TOP_K).transpose(0, 2, 1)
        x_rows = _dispatch(meta, tails, n_used, h1, lp_t, nblk * MOE_BLOCK)
        y_rows = _expert_ffn(
            blk_e, n_used, x_rows,
            w_gate_up[l], b_gate_up[l].reshape(N_EXPERTS, 1, 2 * D_EXPERT),
            w_down[l], b_down[l].reshape(N_EXPERTS, 1, d))
        h = _combine(meta, y_rows, h1, lp_t, gw_t, ln2_g[l].reshape(1, d), ln2_b[l].reshape(1, d)).reshape(b, s, d)
    return h
```

```python
import jax
import jax.numpy as jnp
from jax import lax
from jax.experimental import pallas as pl
from jax.experimental.pallas import tpu as pltpu

F32 = jnp.float32
BF16 = jnp.bfloat16
I32 = jnp.int32
WORD = jnp.uint32

D_MODEL = 1024
HEAD_DIM = 64
ROPE_THETA = 10000.0
A_Q_W = 1024
A_KV_W = 256
A_KV_HEADS = 4
A_GQA = 4
A_WINDOW = 128
B_GROUPS = ((128, 1), (512, 4), (2048, 16))
B_GROUP_W = 256
B_HEADS = 4
B_W = 768
OFF_AK = A_Q_W
OFF_AV = OFF_AK + A_KV_W
OFF_BQ = OFF_AV + A_KV_W
OFF_BK = OFF_BQ + B_W
OFF_BV = OFF_BK + B_W
OFF_G = OFF_BV + B_W
N_EXPERTS = 32
TOP_K = 4
D_EXPERT = 1024
SWIGLU_LIMIT = 7.0
SWIGLU_ALPHA = 1.702
LN_EPS = 1e-5
DEEPNORM_ALPHA = 2.0 ** 0.25
LOG2E = 1.4426950408889634

LANES = 128
SUBLANES = 8
ATTN_BLOCK = 128
SEQ_TILE = 512
ROW_TILE = 512
MOE_BLOCK = 512
FFN_CHUNK = 512
RUN_ALIGN = SUBLANES
PERM_ROWS = ROW_TILE * TOP_K + N_EXPERTS * RUN_ALIGN
PERM_CHUNK = 256
A_SEG_W = A_Q_W + 2 * A_KV_W
B_SEG_W = 3 * B_GROUP_W
QKV_W = A_SEG_W + 3 * B_SEG_W
HALF_W = D_MODEL // 2
ROUTER_PAD = LANES
NEG = -1e30
VMEM_LIMIT = 48 * 1024 * 1024
FFN_VMEM_LIMIT = 56 * 1024 * 1024


def _qkv_rope_kernel(x_ref, w_ref, cos_ref, sin_ref, a_ref, b0_ref, b1_ref, b2_ref, scr_ref):
    tm = x_ref.shape[0]
    xb = x_ref[...].astype(BF16)
    cos = cos_ref[...]
    sin = sin_ref[...]
    lane = lax.broadcasted_iota(I32, (tm, LANES), 1)
    first_half = (lane % HEAD_DIM) < (HEAD_DIM // 2)

    def rope(a):
        partner = jnp.where(first_half, pltpu.roll(a, LANES - 32, 1), pltpu.roll(a, 32, 1))
        return a * cos + partner * sin

    def segment(col0, width, n_rope_chunks):
        acc = jnp.dot(xb, w_ref[:, col0:col0 + width], preferred_element_type=F32)
        pieces = []
        for c in range(width // LANES):
            a = acc[:, c * LANES:(c + 1) * LANES]
            pieces.append(rope(a) if c < n_rope_chunks else a)
        return pieces

    a_ref[...] = jnp.concatenate(segment(0, A_SEG_W, (A_Q_W + A_KV_W) // LANES), axis=1).astype(BF16)
    b0_ref[...] = jnp.concatenate(segment(A_SEG_W, B_SEG_W, 2 * B_GROUP_W // LANES), axis=1).astype(BF16)
    n_chunks = B_SEG_W // LANES
    for out_ref, col0, dil in ((b1_ref, A_SEG_W + B_SEG_W, 4), (b2_ref, A_SEG_W + 2 * B_SEG_W, 16)):
        for k, piece in enumerate(segment(col0, B_SEG_W, 2 * B_GROUP_W // LANES)):
            scr_ref[k] = piece
        for c in range(dil):
            rows = [scr_ref[k, pl.ds(c, tm // dil, stride=dil), :] for k in range(n_chunks)]
            out_ref[c] = jnp.concatenate(rows, axis=1).astype(BF16)


def _qkv_rope(x, w_qkv, cos_t, sin_t):
    b, s, d = x.shape
    tm = SEQ_TILE
    grid = (b, s // tm)
    return pl.pallas_call(
        _qkv_rope_kernel,
        grid=grid,
        in_specs=[
            pl.BlockSpec((None, tm, d), lambda bi, si: (bi, si, 0)),
            pl.BlockSpec((d, QKV_W), lambda bi, si: (0, 0)),
            pl.BlockSpec((tm, LANES), lambda bi, si: (si, 0)),
            pl.BlockSpec((tm, LANES), lambda bi, si: (si, 0)),
        ],
        out_specs=[
            pl.BlockSpec((None, tm, A_SEG_W), lambda bi, si: (bi, si, 0)),
            pl.BlockSpec((None, tm, B_SEG_W), lambda bi, si: (bi, si, 0)),
            pl.BlockSpec((None, 4, tm // 4, B_SEG_W), lambda bi, si: (bi, 0, si, 0)),
            pl.BlockSpec((None, 16, tm // 16, B_SEG_W), lambda bi, si: (bi, 0, si, 0)),
        ],
        out_shape=[
            jax.ShapeDtypeStruct((b, s, A_SEG_W), BF16),
            jax.ShapeDtypeStruct((b, s, B_SEG_W), BF16),
            jax.ShapeDtypeStruct((b, 4, s // 4, B_SEG_W), BF16),
            jax.ShapeDtypeStruct((b, 16, s // 16, B_SEG_W), BF16),
        ],
        scratch_shapes=[pltpu.VMEM((B_SEG_W // LANES, tm, LANES), F32)],
        compiler_params=pltpu.CompilerParams(
            dimension_semantics=("arbitrary", "arbitrary"), vmem_limit_bytes=VMEM_LIMIT),
        name="qkv_rope",
    )(x, w_qkv, cos_t, sin_t)


def _stacked_heads_unit(q, kmat, vmat, bias, sink):
    blk = q.shape[0]
    group = lax.broadcasted_iota(I32, (1, B_GROUP_W), 1) // HEAD_DIM
    zero = jnp.zeros_like(q)
    qs = jnp.concatenate([jnp.where(group == h, q, zero) for h in range(B_HEADS)], axis=0)
    s = lax.dot_general(qs, kmat, (((1,), (1,)), ((), ())), preferred_element_type=F32) + bias
    m = jnp.max(s, axis=-1, keepdims=True)
    if sink is not None:
        m = jnp.maximum(m, sink)
    p = jnp.exp2(s - m)
    l = jnp.sum(p, axis=-1, keepdims=True)
    if sink is not None:
        l = l + jnp.exp2(sink - m)
    o = jnp.dot(p.astype(BF16), vmat, preferred_element_type=F32)
    scale = 1.0 / l
    lse = m + jnp.log2(l)
    out = jnp.zeros((blk, B_GROUP_W), F32)
    scale_b = jnp.zeros((blk, B_GROUP_W), F32)
    lse_b = jnp.zeros((blk, B_GROUP_W), F32)
    for h in range(B_HEADS):
        out = jnp.where(group == h, o[h * blk:(h + 1) * blk], out)
        scale_b = jnp.where(group == h, scale[h * blk:(h + 1) * blk], scale_b)
        lse_b = jnp.where(group == h, lse[h * blk:(h + 1) * blk], lse_b)
    return out * scale_b, lse_b


def _store_lane_chunks(ref, lead, rows, x):
    for c in range(x.shape[1] // LANES):
        ref[(*lead, c, rows, slice(None))] = x[:, c * LANES:(c + 1) * LANES]


def _attention_kernel(sink_ref, a_cur, a_prev, b0_cur, b0_prev, b1_cur, b1_prev, b2_ref,
                      oa_ref, ob_ref, kexp_scr, vexp_scr, kvb_scr, o_scr, lse_scr, o2_scr, lse2_scr,
                      band_scr, causal_scr):
    j = pl.program_id(1)
    blk = ATTN_BLOCK
    nqb = SEQ_TILE // blk
    stack = B_HEADS * blk

    @pl.when(jnp.logical_and(pl.program_id(0) == 0, j == 0))
    def _():
        row = lax.broadcasted_iota(I32, (stack, 2 * blk), 0) % blk
        col = lax.broadcasted_iota(I32, (stack, 2 * blk), 1)
        dist = blk + row - col
        for w, n_back in enumerate((A_WINDOW - 1, B_GROUPS[0][0] // B_GROUPS[0][1])):
            in_band = (dist >= 0) & (dist <= n_back)
            band_scr[2 * w] = jnp.where(in_band, 0.0, NEG)
            band_scr[2 * w + 1] = jnp.where(in_band & (col >= blk), 0.0, NEG)
        r1 = lax.broadcasted_iota(I32, (stack, blk), 0) % blk
        c1 = lax.broadcasted_iota(I32, (stack, blk), 1)
        causal_scr[...] = jnp.where(c1 <= r1, 0.0, NEG)

    @pl.when(j == 0)
    def _():
        def class_body(c, carry):
            t = b2_ref[c]
            out, lse_b = _stacked_heads_unit(t[:, :B_GROUP_W], t[:, B_GROUP_W:2 * B_GROUP_W],
                                             t[:, 2 * B_GROUP_W:], causal_scr[...], None)
            rows = pl.ds(c, blk, stride=16)
            _store_lane_chunks(o2_scr, (), rows, out)
            _store_lane_chunks(lse2_scr, (), rows, lse_b)
            return carry

        lax.fori_loop(0, 16, class_body, 0, unroll=4)

    for src, r0, r1 in ((a_prev, 0, blk), (a_cur, blk, blk + SEQ_TILE)):
        for dst, col0 in ((kexp_scr, A_Q_W), (vexp_scr, A_Q_W + A_KV_W)):
            heads = src[:, col0:col0 + A_KV_W]
            for kvh in range(A_KV_HEADS):
                head = heads[:, kvh * HEAD_DIM:(kvh + 1) * HEAD_DIM]
                dst[r0:r1, kvh * B_GROUP_W:(kvh + 1) * B_GROUP_W] = jnp.concatenate([head] * A_GQA, axis=1)
    kvb_scr[0:blk, :] = b0_prev[:, B_GROUP_W:]
    kvb_scr[blk:, :] = b0_cur[:, B_GROUP_W:]

    row_block = lax.broadcasted_iota(I32, (stack, 1), 0) // blk

    def qblock_body(qb, carry):
        r0 = pl.multiple_of(qb * blk, blk)
        first = jnp.logical_and(j == 0, qb == 0).astype(I32)
        for kvh in range(A_KV_HEADS):
            lanes = slice(kvh * B_GROUP_W, (kvh + 1) * B_GROUP_W)
            sink = jnp.zeros((stack, 1), F32)
            for g in range(A_GQA):
                sink = jnp.where(row_block == g, sink_ref[kvh * A_GQA + g], sink)
            out, _ = _stacked_heads_unit(a_cur[pl.ds(r0, blk), lanes], kexp_scr[pl.ds(r0, 2 * blk), lanes],
                                         vexp_scr[pl.ds(r0, 2 * blk), lanes], band_scr[first], sink)
            oa_ref[pl.ds(r0, blk), lanes] = out.astype(oa_ref.dtype)
        out, lse_b = _stacked_heads_unit(b0_cur[pl.ds(r0, blk), 0:B_GROUP_W],
                                         kvb_scr[pl.ds(r0, 2 * blk), 0:B_GROUP_W],
                                         kvb_scr[pl.ds(r0, 2 * blk), B_GROUP_W:], band_scr[2 + first], None)
        _store_lane_chunks(o_scr, (0,), pl.ds(r0, blk), out)
        _store_lane_chunks(lse_scr, (0,), pl.ds(r0, blk), lse_b)
        return carry

    lax.fori_loop(0, nqb, qblock_body, 0, unroll=True)

    first_class_block = (j == 0).astype(I32)

    def class4_body(c, carry):
        cur = b1_cur[c]
        prev = b1_prev[c]
        kv = jnp.concatenate([prev[:, B_GROUP_W:], cur[:, B_GROUP_W:]], axis=0)
        out, lse_b = _stacked_heads_unit(cur[:, :B_GROUP_W], kv[:, :B_GROUP_W], kv[:, B_GROUP_W:],
                                         band_scr[2 + first_class_block], None)
        rows = pl.ds(c, blk, stride=4)
        _store_lane_chunks(o_scr, (1,), rows, out)
        _store_lane_chunks(lse_scr, (1,), rows, lse_b)
        return carry

    lax.fori_loop(0, 4, class4_body, 0, unroll=True)

    t0 = pl.multiple_of(j * SEQ_TILE, SEQ_TILE)
    for c in range(B_GROUP_W // LANES):
        l0, l1, l2 = lse_scr[0, c], lse_scr[1, c], lse2_scr[c, pl.ds(t0, SEQ_TILE), :]
        mx = jnp.maximum(jnp.maximum(l0, l1), l2)
        w0, w1, w2 = jnp.exp2(l0 - mx), jnp.exp2(l1 - mx), jnp.exp2(l2 - mx)
        num = w0 * o_scr[0, c] + w1 * o_scr[1, c] + w2 * o2_scr[c, pl.ds(t0, SEQ_TILE), :]
        ob_ref[:, c * LANES:(c + 1) * LANES] = (num / (w0 + w1 + w2)).astype(ob_ref.dtype)


def _attention(sinks, qkv_a, qkv_b0, qkv_b1, qkv_b2):
    b, s, _ = qkv_a.shape
    t = SEQ_TILE
    blk = ATTN_BLOCK
    nq = t // blk
    grid = (b, s // t)
    n_chunks = B_GROUP_W // LANES
    return pl.pallas_call(
        _attention_kernel,
        grid=grid,
        in_specs=[
            pl.BlockSpec(memory_space=pltpu.SMEM),
            pl.BlockSpec((None, t, A_SEG_W), lambda bi, j: (bi, j, 0)),
            pl.BlockSpec((None, blk, A_SEG_W), lambda bi, j: (bi, jnp.maximum(j * nq - 1, 0), 0)),
            pl.BlockSpec((None, t, B_SEG_W), lambda bi, j: (bi, j, 0)),
            pl.BlockSpec((None, blk, B_SEG_W), lambda bi, j: (bi, jnp.maximum(j * nq - 1, 0), 0)),
            pl.BlockSpec((None, 4, blk, B_SEG_W), lambda bi, j: (bi, 0, j, 0)),
            pl.BlockSpec((None, 4, blk, B_SEG_W), lambda bi, j: (bi, 0, jnp.maximum(j - 1, 0), 0)),
            pl.BlockSpec((None, 16, blk, B_SEG_W), lambda bi, j: (bi, 0, 0, 0)),
        ],
        out_specs=[
            pl.BlockSpec((None, t, A_Q_W), lambda bi, j: (bi, j, 0)),
            pl.BlockSpec((None, t, B_GROUP_W), lambda bi, j: (bi, j, 0)),
        ],
        out_shape=[
            jax.ShapeDtypeStruct((b, s, A_Q_W), BF16),
            jax.ShapeDtypeStruct((b, s, B_GROUP_W), BF16),
        ],
        scratch_shapes=[
            pltpu.VMEM((blk + t, A_KV_HEADS * B_GROUP_W), BF16),
            pltpu.VMEM((blk + t, A_KV_HEADS * B_GROUP_W), BF16),
            pltpu.VMEM((blk + t, 2 * B_GROUP_W), BF16),
            pltpu.VMEM((2, n_chunks, t, LANES), F32),
            pltpu.VMEM((2, n_chunks, t, LANES), F32),
            pltpu.VMEM((n_chunks, s, LANES), F32),
            pltpu.VMEM((n_chunks, s, LANES), F32),
            pltpu.VMEM((4, B_HEADS * blk, 2 * blk), F32),
            pltpu.VMEM((B_HEADS * blk, blk), F32),
        ],
        compiler_params=pltpu.CompilerParams(
            dimension_semantics=("arbitrary", "arbitrary"), vmem_limit_bytes=VMEM_LIMIT),
        name="banded_attention",
    )(sinks, qkv_a, qkv_a, qkv_b0, qkv_b0, qkv_b1, qkv_b1, qkv_b2)


def _layer_norm(z, g, b):
    mu = jnp.mean(z, axis=-1, keepdims=True)
    zc = z - mu
    var = jnp.mean(zc * zc, axis=-1, keepdims=True)
    return zc * lax.rsqrt(var + LN_EPS) * g + b


def _merge_kernel(x_ref, oa_ref, ob_ref, wg_ref, wa_ref, wb_ref, wo_ref, wr_ref, br_ref, g_ref, b_ref,
                  h_ref, lp_ref, gw_ref, cnt_ref, logits_scr):
    i = pl.program_id(0)

    @pl.when(i == 0)
    def _():
        logits_scr[...] = jnp.zeros_like(logits_scr)

    logits = logits_scr[(i + 1) % 2]

    x = x_ref[...]
    xb = x.astype(BF16)
    gates = jax.nn.sigmoid(jnp.dot(xb, wg_ref[...], preferred_element_type=F32))
    ya = jnp.dot(oa_ref[...], wa_ref[...], preferred_element_type=F32)
    yb = jnp.dot(ob_ref[...], wb_ref[...], preferred_element_type=F32)
    merged = gates[:, :D_MODEL] * ya + gates[:, D_MODEL:] * yb
    mix = jnp.dot(merged.astype(BF16), wo_ref[...], preferred_element_type=F32)
    h = _layer_norm(DEEPNORM_ALPHA * x + mix, g_ref[...], b_ref[...])
    h_ref[...] = h
    logits_scr[i % 2] = jnp.dot(h.astype(BF16), wr_ref[...], preferred_element_type=F32) + br_ref[...]

    tm = logits.shape[0]
    lane = lax.broadcasted_iota(I32, (tm, ROUTER_PAD), 1)
    lane_f = lane.astype(F32)
    v_out = jnp.full((tm, ROUTER_PAD), NEG, F32)
    onehots = []
    lg = logits
    for k in range(TOP_K):
        m = jnp.max(lg, axis=-1, keepdims=True)
        idx = jnp.min(jnp.where(lg == m, lane_f, float(ROUTER_PAD)), axis=-1, keepdims=True)
        chosen = lane_f == idx
        onehots.append(chosen.astype(F32))
        v_out = jnp.where(lane == k, m, v_out)
        lg = jnp.where(chosen, 3.0 * NEG, lg)
    vmax = jnp.max(v_out, axis=-1, keepdims=True)
    ex = jnp.exp(v_out - vmax)
    gw_ref[...] = ex / jnp.sum(ex, axis=-1, keepdims=True)

    routed = onehots[0] + onehots[1] + onehots[2] + onehots[3]
    r_i = lax.broadcasted_iota(I32, (tm, tm), 0)
    c_i = lax.broadcasted_iota(I32, (tm, tm), 1)
    before = jnp.dot((c_i < r_i).astype(BF16), routed.astype(BF16), preferred_element_type=F32)
    cnt = jnp.sum(routed, axis=0, keepdims=True)
    runs = jnp.ceil(cnt * (1.0 / RUN_ALIGN))
    e_r = lax.broadcasted_iota(I32, (ROUTER_PAD, ROUTER_PAD), 0)
    e_c = lax.broadcasted_iota(I32, (ROUTER_PAD, ROUTER_PAD), 1)
    off = jnp.dot(jnp.broadcast_to(runs, (SUBLANES, ROUTER_PAD)).astype(BF16), (e_r < e_c).astype(BF16),
                  preferred_element_type=F32)[0:1] * float(RUN_ALIGN)
    base = before + off
    lp = jnp.zeros((tm, ROUTER_PAD), F32)
    for k in range(TOP_K):
        lp = jnp.where(lane == k, jnp.sum(onehots[k] * base, axis=-1, keepdims=True), lp)
    lp_ref[...] = lp.astype(I32)
    cnt_ref[...] = cnt


def _merge(x2, oa2, ob2, wg, wa, wb, wo, wr, br, ln_g, ln_b):
    t, d = x2.shape
    tm = ROW_TILE
    nt = t // tm
    full = lambda shape: pl.BlockSpec(shape, lambda i: (0, 0))
    dense = lambda i: (jnp.minimum(i, nt - 1), 0)
    routed = lambda i: (jnp.maximum(i - 1, 0), 0)
    return pl.pallas_call(
        _merge_kernel,
        grid=(nt + 1,),
        in_specs=[
            pl.BlockSpec((tm, d), dense),
            pl.BlockSpec((tm, A_Q_W), dense),
            pl.BlockSpec((tm, B_GROUP_W), dense),
            full((d, 2 * d)), full((A_Q_W, d)), full((B_GROUP_W, d)), full((d, d)),
            full((d, ROUTER_PAD)), full((1, ROUTER_PAD)), full((1, d)), full((1, d)),
        ],
        out_specs=[
            pl.BlockSpec((tm, d), dense),
            pl.BlockSpec((tm, ROUTER_PAD), routed),
            pl.BlockSpec((tm, ROUTER_PAD), routed),
            pl.BlockSpec((None, 1, ROUTER_PAD), lambda i: (jnp.maximum(i - 1, 0), 0, 0)),
        ],
        out_shape=[
            jax.ShapeDtypeStruct((t, d), F32),
            jax.ShapeDtypeStruct((t, ROUTER_PAD), I32),
            jax.ShapeDtypeStruct((t, ROUTER_PAD), F32),
            jax.ShapeDtypeStruct((t // tm, 1, ROUTER_PAD), F32),
        ],
        scratch_shapes=[pltpu.VMEM((2, tm, ROUTER_PAD), F32)],
        compiler_params=pltpu.CompilerParams(
            dimension_semantics=("arbitrary",), vmem_limit_bytes=VMEM_LIMIT),
        name="merge_ln_router",
    )(x2, oa2, ob2, wg, wa, wb, wo, wr, br, ln_g, ln_b)


def _pack_rows(v, already_bf16=False):
    vb = v if already_bf16 else v.astype(BF16).astype(F32)
    lo = lax.bitcast_convert_type(vb[:, :HALF_W], WORD) >> 16
    hi = lax.bitcast_convert_type(vb[:, HALF_W:], WORD) & jnp.uint32(0xFFFF0000)
    return lo | hi


def _unpack_rows(w):
    lo = lax.bitcast_convert_type(w << 16, F32)
    hi = lax.bitcast_convert_type(w & jnp.uint32(0xFFFF0000), F32)
    return jnp.concatenate([lo, hi], axis=1).astype(BF16)


def _run_copy(tile, e, meta, vmem_rows, hbm_rows, sem, to_hbm):
    run_start_ref, n8_ref, off_ref = meta
    idx = tile * N_EXPERTS + e
    n = pl.multiple_of(n8_ref[idx], RUN_ALIGN)
    local = vmem_rows.at[pl.ds(pl.multiple_of(off_ref[idx], RUN_ALIGN), n)]
    remote = hbm_rows.at[pl.ds(pl.multiple_of(run_start_ref[idx], RUN_ALIGN), n)]
    return (pltpu.make_async_copy(local, remote, sem) if to_hbm
            else pltpu.make_async_copy(remote, local, sem)), n


def _start_runs(tile, meta, vmem_rows, hbm_rows, sem, to_hbm):
    def body(e, carry):
        copy, n = _run_copy(tile, e, meta, vmem_rows, hbm_rows, sem, to_hbm)

        @pl.when(n > 0)
        def _():
            copy.start()
        return carry
    lax.fori_loop(0, N_EXPERTS, body, 0, unroll=4)


def _wait_runs(tile, meta, vmem_rows, hbm_rows, sem, to_hbm):
    _, n8_ref, off_ref = meta
    last = tile * N_EXPERTS + N_EXPERTS - 1
    total = pl.multiple_of(off_ref[last] + n8_ref[last], RUN_ALIGN)
    local = vmem_rows.at[pl.ds(0, total)]
    remote = hbm_rows.at[pl.ds(0, total)]
    (pltpu.make_async_copy(local, remote, sem) if to_hbm else pltpu.make_async_copy(remote, local, sem)).wait()


def _dispatch_kernel(run_start_ref, n8_ref, off_ref, tail_start_ref, tail_len_ref, nused_ref,
                     h_ref, lpt_ref, x_hbm, buf, zbuf, sem, zsem):
    i = pl.program_id(0)
    nt = pl.num_programs(0)
    slot = i % 2
    meta = (run_start_ref, n8_ref, off_ref)

    @pl.when(i >= 2)
    def _():
        _wait_runs(i - 2, meta, buf.at[slot], x_hbm, sem.at[slot], True)

    hb = h_ref[...].astype(BF16)
    lp16 = lpt_ref[...].astype(jnp.int16)
    one = jnp.ones((PERM_CHUNK, ROW_TILE), BF16)
    for c in range(PERM_ROWS // PERM_CHUNK):
        r0 = c * PERM_CHUNK
        rows = (lax.broadcasted_iota(I32, (PERM_CHUNK, ROW_TILE), 0) + r0).astype(jnp.int16)
        perm = jnp.zeros((PERM_CHUNK, ROW_TILE), BF16)
        for k in range(TOP_K):
            perm = jnp.where(rows == lp16[k:k + 1, :], one, perm)
        buf[slot, r0:r0 + PERM_CHUNK, :] = _pack_rows(jnp.dot(perm, hb, preferred_element_type=F32),
                                                      already_bf16=True)

    _start_runs(i, meta, buf.at[slot], x_hbm, sem.at[slot], True)

    @pl.when(i == nt - 1)
    def _():
        zbuf[...] = jnp.zeros_like(zbuf)

        def tail_copy(e):
            n = pl.multiple_of(tail_len_ref[e], RUN_ALIGN)
            dst = x_hbm.at[pl.ds(pl.multiple_of(tail_start_ref[e], RUN_ALIGN), n)]
            return pltpu.make_async_copy(zbuf.at[pl.ds(0, n)], dst, zsem), n

        def start_tail(e, carry):
            copy, n = tail_copy(e)

            @pl.when(n > 0)
            def _():
                copy.start()
            return carry

        def wait_tail(e, carry):
            copy, n = tail_copy(e)

            @pl.when(n > 0)
            def _():
                copy.wait()
            return carry

        lax.fori_loop(0, N_EXPERTS, start_tail, 0)
        lax.fori_loop(0, N_EXPERTS, wait_tail, 0)

        def spare_copy(blk_i):
            dst = x_hbm.at[pl.ds(pl.multiple_of(blk_i * MOE_BLOCK, MOE_BLOCK), MOE_BLOCK)]
            return pltpu.make_async_copy(zbuf, dst, zsem)

        def start_spare(blk_i, carry):
            spare_copy(blk_i).start()
            return carry

        def wait_spare(blk_i, carry):
            spare_copy(blk_i).wait()
            return carry

        n_blocks = x_hbm.shape[0] // MOE_BLOCK
        lax.fori_loop(nused_ref[0], n_blocks, start_spare, 0)
        lax.fori_loop(nused_ref[0], n_blocks, wait_spare, 0)

        @pl.when(nt >= 2)
        def _():
            _wait_runs(i - 1, meta, buf.at[1 - slot], x_hbm, sem.at[1 - slot], True)
        _wait_runs(i, meta, buf.at[slot], x_hbm, sem.at[slot], True)


def _dispatch(meta, tails, n_used, h2, lp_t, n_rows):
    t, d = h2.shape
    nt = t // ROW_TILE
    grid_spec = pltpu.PrefetchScalarGridSpec(
        num_scalar_prefetch=6,
        grid=(nt,),
        in_specs=[
            pl.BlockSpec((ROW_TILE, d), lambda i, *_: (i, 0)),
            pl.BlockSpec((None, TOP_K, ROW_TILE), lambda i, *_: (i, 0, 0)),
        ],
        out_specs=pl.BlockSpec(memory_space=pl.ANY),
        scratch_shapes=[
            pltpu.VMEM((2, PERM_ROWS, HALF_W), WORD),
            pltpu.VMEM((MOE_BLOCK, HALF_W), WORD),
            pltpu.SemaphoreType.DMA((2,)),
            pltpu.SemaphoreType.DMA(()),
        ],
    )
    return pl.pallas_call(
        _dispatch_kernel,
        grid_spec=grid_spec,
        out_shape=jax.ShapeDtypeStruct((n_rows, HALF_W), WORD),
        compiler_params=pltpu.CompilerParams(
            dimension_semantics=("arbitrary",), vmem_limit_bytes=VMEM_LIMIT, has_side_effects=True),
        name="dispatch",
    )(*meta, *tails, n_used, h2, lp_t)


def _ffn_kernel(blk_e_ref, nused_ref, x_ref, wgu_ref, bgu_ref, wdn_ref, bdn_ref, y_ref, wgu_bf, wdn_bf):
    i = pl.program_id(0)
    used = i < nused_ref[0]
    new_expert = jnp.logical_or(i == 0, blk_e_ref[i] != blk_e_ref[jnp.maximum(i - 1, 0)])

    @pl.when(jnp.logical_and(used, new_expert))
    def _():
        wgu_bf[...] = wgu_ref[...].astype(BF16)
        wdn_bf[...] = wdn_ref[...].astype(BF16)

    @pl.when(used)
    def _():
        xb = _unpack_rows(x_ref[...])
        acc = jnp.zeros((MOE_BLOCK, D_MODEL), F32)
        for c in range(D_EXPERT // FFN_CHUNK):
            lo = c * FFN_CHUNK
            gate = jnp.dot(xb, wgu_bf[:, lo:lo + FFN_CHUNK], preferred_element_type=F32)
            gate = gate + bgu_ref[:, lo:lo + FFN_CHUNK]
            up = jnp.dot(xb, wgu_bf[:, D_EXPERT + lo:D_EXPERT + lo + FFN_CHUNK], preferred_element_type=F32)
            up = up + bgu_ref[:, D_EXPERT + lo:D_EXPERT + lo + FFN_CHUNK]
            gate = jnp.minimum(gate, SWIGLU_LIMIT)
            up = jnp.clip(up, -SWIGLU_LIMIT, SWIGLU_LIMIT)
            hmid = (up + 1.0) * gate * jax.nn.sigmoid(SWIGLU_ALPHA * gate)
            acc = acc + jnp.dot(hmid.astype(BF16), wdn_bf[lo:lo + FFN_CHUNK, :], preferred_element_type=F32)
        y_ref[...] = _pack_rows(acc + bdn_ref[...])

    @pl.when(i >= nused_ref[0])
    def _():
        y_ref[...] = jnp.zeros_like(y_ref)


def _expert_ffn(blk_e, n_used, x_rows, wgu, bgu, wdn, bdn):
    nblk = x_rows.shape[0] // MOE_BLOCK
    d = D_MODEL
    grid_spec = pltpu.PrefetchScalarGridSpec(
        num_scalar_prefetch=2,
        grid=(nblk,),
        in_specs=[
            pl.BlockSpec((MOE_BLOCK, HALF_W), lambda i, be, nu: (jnp.minimum(i, nu[0] - 1), 0)),
            pl.BlockSpec((None, d, 2 * D_EXPERT), lambda i, be, nu: (be[i], 0, 0)),
            pl.BlockSpec((None, 1, 2 * D_EXPERT), lambda i, be, nu: (be[i], 0, 0)),
            pl.BlockSpec((None, D_EXPERT, d), lambda i, be, nu: (be[i], 0, 0)),
            pl.BlockSpec((None, 1, d), lambda i, be, nu: (be[i], 0, 0)),
        ],
        out_specs=pl.BlockSpec((MOE_BLOCK, HALF_W), lambda i, be, nu: (i, 0)),
        scratch_shapes=[pltpu.VMEM((d, 2 * D_EXPERT), BF16), pltpu.VMEM((D_EXPERT, d), BF16)],
    )
    return pl.pallas_call(
        _ffn_kernel,
        grid_spec=grid_spec,
        out_shape=jax.ShapeDtypeStruct((nblk * MOE_BLOCK, HALF_W), WORD),
        compiler_params=pltpu.CompilerParams(
            dimension_semantics=("arbitrary",), vmem_limit_bytes=FFN_VMEM_LIMIT),
        name="expert_ffn",
    )(blk_e, n_used, x_rows, wgu, bgu, wdn, bdn)


def _combine_kernel(run_start_ref, n8_ref, off_ref, y_hbm, h_ref, lpt_ref, gwt_ref, g_ref, b_ref,
                    out_ref, ybuf, wperm_scr, ysel_scr, sem):
    i = pl.program_id(0)
    nt = pl.num_programs(0)
    slot = i % 2
    meta = (run_start_ref, n8_ref, off_ref)

    @pl.when(i == 0)
    def _():
        ybuf[...] = jnp.zeros_like(ybuf)
        _start_runs(0, meta, ybuf.at[0], y_hbm, sem.at[0], False)

    @pl.when(i + 1 < nt)
    def _():
        _start_runs(i + 1, meta, ybuf.at[1 - slot], y_hbm, sem.at[1 - slot], False)

    _wait_runs(i, meta, ybuf.at[slot], y_hbm, sem.at[slot], False)

    lp16 = lpt_ref[...].astype(jnp.int16)
    gw16 = gwt_ref[...].astype(BF16)
    for c in range(PERM_ROWS // PERM_CHUNK):
        r0 = c * PERM_CHUNK
        rows = (lax.broadcasted_iota(I32, (PERM_CHUNK, ROW_TILE), 0) + r0).astype(jnp.int16)
        wperm = jnp.zeros((PERM_CHUNK, ROW_TILE), BF16)
        for k in range(TOP_K):
            wperm = jnp.where(rows == lp16[k:k + 1, :], gw16[k:k + 1, :], wperm)
        wperm_scr[r0:r0 + PERM_CHUNK, :] = wperm
        ysel_scr[r0:r0 + PERM_CHUNK, :] = _unpack_rows(ybuf[slot, r0:r0 + PERM_CHUNK, :])
    half = ROW_TILE // 2
    for r in range(2):
        tok = slice(r * half, (r + 1) * half)
        ffn = lax.dot_general(wperm_scr[:, tok], ysel_scr[...], (((0,), (0,)), ((), ())),
                              preferred_element_type=F32)
        out_ref[tok, :] = _layer_norm(DEEPNORM_ALPHA * h_ref[tok, :] + ffn, g_ref[...], b_ref[...])


def _combine(meta, y_rows, h2, lp_t, gw_t, ln_g, ln_b):
    t, d = h2.shape
    nt = t // ROW_TILE
    grid_spec = pltpu.PrefetchScalarGridSpec(
        num_scalar_prefetch=3,
        grid=(nt,),
        in_specs=[
            pl.BlockSpec(memory_space=pl.ANY),
            pl.BlockSpec((ROW_TILE, d), lambda i, *_: (i, 0)),
            pl.BlockSpec((None, TOP_K, ROW_TILE), lambda i, *_: (i, 0, 0)),
            pl.BlockSpec((None, TOP_K, ROW_TILE), lambda i, *_: (i, 0, 0)),
            pl.BlockSpec((1, d), lambda i, *_: (0, 0)),
            pl.BlockSpec((1, d), lambda i, *_: (0, 0)),
        ],
        out_specs=pl.BlockSpec((ROW_TILE, d), lambda i, *_: (i, 0)),
        scratch_shapes=[pltpu.VMEM((2, PERM_ROWS, HALF_W), WORD), pltpu.VMEM((PERM_ROWS, ROW_TILE), BF16),
                        pltpu.VMEM((PERM_ROWS, d), BF16), pltpu.SemaphoreType.DMA((2,))],
    )
    return pl.pallas_call(
        _combine_kernel,
        grid_spec=grid_spec,
        out_shape=jax.ShapeDtypeStruct((t, d), F32),
        compiler_params=pltpu.CompilerParams(
            dimension_semantics=("arbitrary",), vmem_limit_bytes=VMEM_LIMIT),
        name="combine_ln",
    )(*meta, y_rows, h2, lp_t, gw_t, ln_g, ln_b)


def _max_blocks(t):
    nt = t // ROW_TILE
    rows = t * TOP_K + nt * N_EXPERTS * (RUN_ALIGN - 1) + N_EXPERTS * (MOE_BLOCK - RUN_ALIGN)
    return -(-rows // MOE_BLOCK)


def _layout(counts, nblk):
    n8 = (counts + RUN_ALIGN - 1) // RUN_ALIGN * RUN_ALIGN
    total = jnp.sum(n8, axis=0)
    padded = (total + MOE_BLOCK - 1) // MOE_BLOCK * MOE_BLOCK
    e_end = jnp.cumsum(padded)
    e_start = e_end - padded
    run_start = e_start[None, :] + jnp.cumsum(n8, axis=0) - n8
    off = jnp.cumsum(n8, axis=1) - n8
    blk_row0 = jnp.arange(nblk, dtype=I32) * MOE_BLOCK
    blk_e = jnp.minimum(jnp.sum((e_end[None, :] <= blk_row0[:, None]).astype(I32), axis=1), N_EXPERTS - 1)
    n_used = (e_end[-1] // MOE_BLOCK).reshape(1)
    meta = tuple(a.reshape(-1).astype(I32) for a in (run_start, n8, off))
    tails = ((e_start + total).astype(I32), (padded - total).astype(I32))
    return meta, tails, blk_e.astype(I32), n_used.astype(I32)


def kernel(x, w_in, attn_sinks, w_branch_a, w_branch_b, w_out, ln1_g, ln1_b, w_router, b_router,
           w_gate_up, b_gate_up, w_down, b_down, ln2_g, ln2_b):
    depth = w_in.shape[0]
    b, s, d = x.shape
    t = b * s
    nt = t // ROW_TILE

    pos = jnp.arange(s, dtype=F32)
    inv_freq = ROPE_THETA ** (-jnp.arange(0, HEAD_DIM, 2, dtype=F32) / HEAD_DIM)
    ang = pos[:, None] * inv_freq[None, :]
    cos32, sin32 = jnp.cos(ang), jnp.sin(ang)
    cos_t = jnp.tile(cos32, (1, LANES // (HEAD_DIM // 2)))
    sin_t = jnp.tile(jnp.concatenate([-sin32, sin32], axis=1), (1, LANES // HEAD_DIM))

    h = x
    for l in range(depth):
        wi = w_in[l]
        qscale = HEAD_DIM ** -0.5 * LOG2E
        cols = [wi[:, :A_Q_W] * qscale, wi[:, OFF_AK:OFF_BQ]]
        for g in range(len(B_GROUPS)):
            cols += [wi[:, OFF_BQ + g * B_GROUP_W:OFF_BQ + (g + 1) * B_GROUP_W] * qscale,
                     wi[:, OFF_BK + g * B_GROUP_W:OFF_BK + (g + 1) * B_GROUP_W],
                     wi[:, OFF_BV + g * B_GROUP_W:OFF_BV + (g + 1) * B_GROUP_W]]
        w_qkv = jnp.concatenate(cols, axis=1).astype(BF16)

        qkv_a, qkv_b0, qkv_b1, qkv_b2 = _qkv_rope(h, w_qkv, cos_t, sin_t)
        oa, ob = _attention(attn_sinks[l].astype(F32) * LOG2E, qkv_a, qkv_b0, qkv_b1, qkv_b2)

        wr = jnp.zeros((d, ROUTER_PAD), F32).at[:, :N_EXPERTS].set(w_router[l]).astype(BF16)
        br = jnp.full((1, ROUTER_PAD), NEG, F32).at[0, :N_EXPERTS].set(b_router[l].astype(F32))
        h1, lp, gw, counts = _merge(
            h.reshape(t, d), oa.reshape(t, A_Q_W), ob.reshape(t, B_GROUP_W),
            wi[:, OFF_G:].astype(BF16), w_branch_a[l].astype(BF16), w_branch_b[l].astype(BF16),
            w_out[l].astype(BF16), wr, br, ln1_g[l].reshape(1, d), ln1_b[l].reshape(1, d))

        nblk = _max_blocks(t)
        meta, tails, blk_e, n_used = _layout(counts.reshape(nt, ROUTER_PAD)[:, :N_EXPERTS].astype(I32), nblk)
        lp_t = lp[:, :TOP_K].reshape(nt, ROW_TILE, TOP_K).transpose(0, 2, 1)
        gw_t = gw[:, :TOP_K].reshape(nt, ROW_TILE, TOP_K).transpose(0, 2, 1)
        x_rows = _dispatch(meta, tails, n_used, h1, lp_t, nblk * MOE_BLOCK)
        y_rows = _expert_ffn(
            blk_e, n_used, x_rows,
            w_gate_up[l], b_gate_up[l].reshape(N_EXPERTS, 1, 2 * D_EXPERT),
            w_down[l], b_down[l].reshape(N_EXPERTS, 1, d))
        h = _combine(meta, y_rows, h1, lp_t, gw_t, ln2_g[l].reshape(1, d), ln2_b[l].reshape(1, d)).reshape(b, s, d)
    return h
```

```python
import jax
import jax.numpy as jnp
from jax import lax
from jax.experimental import pallas as pl
from jax.experimental.pallas import tpu as pltpu

F32 = jnp.float32
BF16 = jnp.bfloat16
I32 = jnp.int32
WORD = jnp.uint32

D_MODEL = 1024
HEAD_DIM = 64
ROPE_THETA = 10000.0
A_Q_W = 1024
A_KV_W = 256
A_KV_HEADS = 4
A_GQA = 4
A_WINDOW = 128
B_GROUPS = ((128, 1), (512, 4), (2048, 16))
B_GROUP_W = 256
B_HEADS = 4
B_W = 768
OFF_AK = A_Q_W
OFF_AV = OFF_AK + A_KV_W
OFF_BQ = OFF_AV + A_KV_W
OFF_BK = OFF_BQ + B_W
OFF_BV = OFF_BK + B_W
OFF_G = OFF_BV + B_W
N_EXPERTS = 32
TOP_K = 4
D_EXPERT = 1024
SWIGLU_LIMIT = 7.0
SWIGLU_ALPHA = 1.702
LN_EPS = 1e-5
DEEPNORM_ALPHA = 2.0 ** 0.25
LOG2E = 1.4426950408889634

LANES = 128
SUBLANES = 8
ATTN_BLOCK = 128
SEQ_TILE = 512
ROW_TILE = 512
MOE_BLOCK = 512
FFN_CHUNK = 512
RUN_ALIGN = SUBLANES
PERM_ROWS = ROW_TILE * TOP_K + N_EXPERTS * RUN_ALIGN
PERM_CHUNK = 256
A_SEG_W = A_Q_W + 2 * A_KV_W
B_SEG_W = 3 * B_GROUP_W
QKV_W = A_SEG_W + 3 * B_SEG_W
HALF_W = D_MODEL // 2
ROUTER_PAD = LANES
NEG = -1e30
VMEM_LIMIT = 48 * 1024 * 1024
FFN_VMEM_LIMIT = 56 * 1024 * 1024


def _qkv_rope_kernel(x_ref, w_ref, cos_ref, sin_ref, a_ref, b0_ref, b1_ref, b2_ref, scr_ref):
    tm = x_ref.shape[0]
    xb = x_ref[...].astype(BF16)
    cos = cos_ref[...]
    sin = sin_ref[...]
    lane = lax.broadcasted_iota(I32, (tm, LANES), 1)
    first_half = (lane % HEAD_DIM) < (HEAD_DIM // 2)

    def rope(a):
        partner = jnp.where(first_half, pltpu.roll(a, LANES - 32, 1), pltpu.roll(a, 32, 1))
        return a * cos + partner * sin

    def segment(col0, width, n_rope_chunks):
        acc = jnp.dot(xb, w_ref[:, col0:col0 + width], preferred_element_type=F32)
        pieces = []
        for c in range(width // LANES):
            a = acc[:, c * LANES:(c + 1) * LANES]
            pieces.append(rope(a) if c < n_rope_chunks else a)
        return pieces

    a_ref[...] = jnp.concatenate(segment(0, A_SEG_W, (A_Q_W + A_KV_W) // LANES), axis=1).astype(BF16)
    b0_ref[...] = jnp.concatenate(segment(A_SEG_W, B_SEG_W, 2 * B_GROUP_W // LANES), axis=1).astype(BF16)
    n_chunks = B_SEG_W // LANES
    for out_ref, col0, dil in ((b1_ref, A_SEG_W + B_SEG_W, 4), (b2_ref, A_SEG_W + 2 * B_SEG_W, 16)):
        for k, piece in enumerate(segment(col0, B_SEG_W, 2 * B_GROUP_W // LANES)):
            scr_ref[k] = piece
        for c in range(dil):
            rows = [scr_ref[k, pl.ds(c, tm // dil, stride=dil), :] for k in range(n_chunks)]
            out_ref[c] = jnp.concatenate(rows, axis=1).astype(BF16)


def _qkv_rope(x, w_qkv, cos_t, sin_t):
    b, s, d = x.shape
    tm = SEQ_TILE
    grid = (b, s // tm)
    return pl.pallas_call(
        _qkv_rope_kernel,
        grid=grid,
        in_specs=[
            pl.BlockSpec((None, tm, d), lambda bi, si: (bi, si, 0)),
            pl.BlockSpec((d, QKV_W), lambda bi, si: (0, 0)),
            pl.BlockSpec((tm, LANES), lambda bi, si: (si, 0)),
            pl.BlockSpec((tm, LANES), lambda bi, si: (si, 0)),
        ],
        out_specs=[
            pl.BlockSpec((None, tm, A_SEG_W), lambda bi, si: (bi, si, 0)),
            pl.BlockSpec((None, tm, B_SEG_W), lambda bi, si: (bi, si, 0)),
            pl.BlockSpec((None, 4, tm // 4, B_SEG_W), lambda bi, si: (bi, 0, si, 0)),
            pl.BlockSpec((None, 16, tm // 16, B_SEG_W), lambda bi, si: (bi, 0, si, 0)),
        ],
        out_shape=[
            jax.ShapeDtypeStruct((b, s, A_SEG_W), BF16),
            jax.ShapeDtypeStruct((b, s, B_SEG_W), BF16),
            jax.ShapeDtypeStruct((b, 4, s // 4, B_SEG_W), BF16),
            jax.ShapeDtypeStruct((b, 16, s // 16, B_SEG_W), BF16),
        ],
        scratch_shapes=[pltpu.VMEM((B_SEG_W // LANES, tm, LANES), F32)],
        compiler_params=pltpu.CompilerParams(
            dimension_semantics=("arbitrary", "arbitrary"), vmem_limit_bytes=VMEM_LIMIT),
        name="qkv_rope",
    )(x, w_qkv, cos_t, sin_t)


def _stacked_heads_unit(q, kmat, vmat, bias, sink_slot=False):
    blk = q.shape[0]
    group = lax.broadcasted_iota(I32, (1, B_GROUP_W), 1) // HEAD_DIM
    zero = jnp.zeros_like(q)
    qs = jnp.concatenate([jnp.where(group == h, q, zero) for h in range(B_HEADS)], axis=0)
    if sink_slot:
        tile = 2 * SUBLANES
        not_slot = lax.broadcasted_iota(I32, (tile, B_GROUP_W), 0) > 0
        kmat = jnp.concatenate([jnp.where(not_slot, kmat[:tile], jnp.zeros_like(kmat[:tile])), kmat[tile:]], axis=0)
        vmat = jnp.concatenate([jnp.where(not_slot, vmat[:tile], jnp.zeros_like(vmat[:tile])), vmat[tile:]], axis=0)
    s = lax.dot_general(qs, kmat, (((1,), (1,)), ((), ())), preferred_element_type=F32) + bias
    m = jnp.max(s, axis=-1, keepdims=True)
    p = jnp.exp2(s - m)
    l = jnp.sum(p, axis=-1, keepdims=True)
    o = jnp.dot(p.astype(BF16), vmat, preferred_element_type=F32)
    scale = 1.0 / l
    lse = m + jnp.log2(l)
    out = jnp.zeros((blk, B_GROUP_W), F32)
    scale_b = jnp.zeros((blk, B_GROUP_W), F32)
    lse_b = jnp.zeros((blk, B_GROUP_W), F32)
    for h in range(B_HEADS):
        out = jnp.where(group == h, o[h * blk:(h + 1) * blk], out)
        scale_b = jnp.where(group == h, scale[h * blk:(h + 1) * blk], scale_b)
        lse_b = jnp.where(group == h, lse[h * blk:(h + 1) * blk], lse_b)
    return out * scale_b, lse_b


def _store_lane_chunks(ref, lead, rows, x):
    for c in range(x.shape[1] // LANES):
        ref[(*lead, c, rows, slice(None))] = x[:, c * LANES:(c + 1) * LANES]


def _attention_kernel(sink_ref, a_cur, a_prev, b0_cur, b0_prev, b1_cur, b1_prev, b2_ref,
                      oa_ref, ob_ref, kexp_scr, vexp_scr, kvb_scr, o_scr, lse_scr, o2_scr, lse2_scr,
                      band_scr, sinkband_scr, causal_scr):
    j = pl.program_id(1)
    blk = ATTN_BLOCK
    nqb = SEQ_TILE // blk
    stack = B_HEADS * blk

    @pl.when(jnp.logical_and(pl.program_id(0) == 0, j == 0))
    def _():
        row = lax.broadcasted_iota(I32, (stack, 2 * blk), 0) % blk
        col = lax.broadcasted_iota(I32, (stack, 2 * blk), 1)
        row_block = lax.broadcasted_iota(I32, (stack, 2 * blk), 0) // blk
        dist = blk + row - col
        in_band = (dist >= 0) & (dist <= B_GROUPS[0][0] // B_GROUPS[0][1])
        band_scr[0] = jnp.where(in_band, 0.0, NEG)
        band_scr[1] = jnp.where(in_band & (col >= blk), 0.0, NEG)
        in_band = (dist >= 0) & (dist <= A_WINDOW - 1)
        for kvh in range(A_KV_HEADS):
            sink = jnp.zeros((stack, 2 * blk), F32)
            for g in range(A_GQA):
                sink = jnp.where(row_block == g, sink_ref[kvh * A_GQA + g], sink)
            sinkband_scr[2 * kvh] = jnp.where(col == 0, sink, jnp.where(in_band, 0.0, NEG))
            sinkband_scr[2 * kvh + 1] = jnp.where(col == 0, sink, jnp.where(in_band & (col >= blk), 0.0, NEG))
        r1 = lax.broadcasted_iota(I32, (stack, blk), 0) % blk
        c1 = lax.broadcasted_iota(I32, (stack, blk), 1)
        causal_scr[...] = jnp.where(c1 <= r1, 0.0, NEG)

    @pl.when(j == 0)
    def _():
        def class_body(c, carry):
            t = b2_ref[c]
            out, lse_b = _stacked_heads_unit(t[:, :B_GROUP_W], t[:, B_GROUP_W:2 * B_GROUP_W],
                                             t[:, 2 * B_GROUP_W:], causal_scr[...])
            rows = pl.ds(c, blk, stride=16)
            _store_lane_chunks(o2_scr, (), rows, out)
            _store_lane_chunks(lse2_scr, (), rows, lse_b)
            return carry

        lax.fori_loop(0, 16, class_body, 0, unroll=4)

    for src, r0, r1 in ((a_prev, 0, blk), (a_cur, blk, blk + SEQ_TILE)):
        for dst, col0 in ((kexp_scr, A_Q_W), (vexp_scr, A_Q_W + A_KV_W)):
            heads = src[:, col0:col0 + A_KV_W]
            for kvh in range(A_KV_HEADS):
                head = heads[:, kvh * HEAD_DIM:(kvh + 1) * HEAD_DIM]
                dst[r0:r1, kvh * B_GROUP_W:(kvh + 1) * B_GROUP_W] = jnp.concatenate([head] * A_GQA, axis=1)
    kvb_scr[0:blk, :] = b0_prev[:, B_GROUP_W:]
    kvb_scr[blk:, :] = b0_cur[:, B_GROUP_W:]

    def qblock_body(qb, carry):
        r0 = pl.multiple_of(qb * blk, blk)
        first = jnp.logical_and(j == 0, qb == 0).astype(I32)
        for kvh in range(A_KV_HEADS):
            lanes = slice(kvh * B_GROUP_W, (kvh + 1) * B_GROUP_W)
            out, _ = _stacked_heads_unit(a_cur[pl.ds(r0, blk), lanes], kexp_scr[pl.ds(r0, 2 * blk), lanes],
                                         vexp_scr[pl.ds(r0, 2 * blk), lanes], sinkband_scr[2 * kvh + first],
                                         sink_slot=True)
            oa_ref[pl.ds(r0, blk), lanes] = out.astype(oa_ref.dtype)
        out, lse_b = _stacked_heads_unit(b0_cur[pl.ds(r0, blk), 0:B_GROUP_W],
                                         kvb_scr[pl.ds(r0, 2 * blk), 0:B_GROUP_W],
                                         kvb_scr[pl.ds(r0, 2 * blk), B_GROUP_W:], band_scr[first])
        _store_lane_chunks(o_scr, (0,), pl.ds(r0, blk), out)
        _store_lane_chunks(lse_scr, (0,), pl.ds(r0, blk), lse_b)
        return carry

    lax.fori_loop(0, nqb, qblock_body, 0, unroll=True)

    first_class_block = (j == 0).astype(I32)

    def class4_body(c, carry):
        cur = b1_cur[c]
        prev = b1_prev[c]
        kv = jnp.concatenate([prev[:, B_GROUP_W:], cur[:, B_GROUP_W:]], axis=0)
        out, lse_b = _stacked_heads_unit(cur[:, :B_GROUP_W], kv[:, :B_GROUP_W], kv[:, B_GROUP_W:],
                                         band_scr[first_class_block])
        rows = pl.ds(c, blk, stride=4)
        _store_lane_chunks(o_scr, (1,), rows, out)
        _store_lane_chunks(lse_scr, (1,), rows, lse_b)
        return carry

    lax.fori_loop(0, 4, class4_body, 0, unroll=True)

    t0 = pl.multiple_of(j * SEQ_TILE, SEQ_TILE)
    for c in range(B_GROUP_W // LANES):
        l0, l1, l2 = lse_scr[0, c], lse_scr[1, c], lse2_scr[c, pl.ds(t0, SEQ_TILE), :]
        mx = jnp.maximum(jnp.maximum(l0, l1), l2)
        w0, w1, w2 = jnp.exp2(l0 - mx), jnp.exp2(l1 - mx), jnp.exp2(l2 - mx)
        num = w0 * o_scr[0, c] + w1 * o_scr[1, c] + w2 * o2_scr[c, pl.ds(t0, SEQ_TILE), :]
        ob_ref[:, c * LANES:(c + 1) * LANES] = (num / (w0 + w1 + w2)).astype(ob_ref.dtype)


def _attention(sinks, qkv_a, qkv_b0, qkv_b1, qkv_b2):
    b, s, _ = qkv_a.shape
    t = SEQ_TILE
    blk = ATTN_BLOCK
    nq = t // blk
    grid = (b, s // t)
    n_chunks = B_GROUP_W // LANES
    return pl.pallas_call(
        _attention_kernel,
        grid=grid,
        in_specs=[
            pl.BlockSpec(memory_space=pltpu.SMEM),
            pl.BlockSpec((None, t, A_SEG_W), lambda bi, j: (bi, j, 0)),
            pl.BlockSpec((None, blk, A_SEG_W), lambda bi, j: (bi, jnp.maximum(j * nq - 1, 0), 0)),
            pl.BlockSpec((None, t, B_SEG_W), lambda bi, j: (bi, j, 0)),
            pl.BlockSpec((None, blk, B_SEG_W), lambda bi, j: (bi, jnp.maximum(j * nq - 1, 0), 0)),
            pl.BlockSpec((None, 4, blk, B_SEG_W), lambda bi, j: (bi, 0, j, 0)),
            pl.BlockSpec((None, 4, blk, B_SEG_W), lambda bi, j: (bi, 0, jnp.maximum(j - 1, 0), 0)),
            pl.BlockSpec((None, 16, blk, B_SEG_W), lambda bi, j: (bi, 0, 0, 0)),
        ],
        out_specs=[
            pl.BlockSpec((None, t, A_Q_W), lambda bi, j: (bi, j, 0)),
            pl.BlockSpec((None, t, B_GROUP_W), lambda bi, j: (bi, j, 0)),
        ],
        out_shape=[
            jax.ShapeDtypeStruct((b, s, A_Q_W), BF16),
            jax.ShapeDtypeStruct((b, s, B_GROUP_W), BF16),
        ],
        scratch_shapes=[
            pltpu.VMEM((blk + t, A_KV_HEADS * B_GROUP_W), BF16),
            pltpu.VMEM((blk + t, A_KV_HEADS * B_GROUP_W), BF16),
            pltpu.VMEM((blk + t, 2 * B_GROUP_W), BF16),
            pltpu.VMEM((2, n_chunks, t, LANES), F32),
            pltpu.VMEM((2, n_chunks, t, LANES), F32),
            pltpu.VMEM((n_chunks, s, LANES), F32),
            pltpu.VMEM((n_chunks, s, LANES), F32),
            pltpu.VMEM((2, B_HEADS * blk, 2 * blk), F32),
            pltpu.VMEM((2 * A_KV_HEADS, B_HEADS * blk, 2 * blk), F32),
            pltpu.VMEM((B_HEADS * blk, blk), F32),
        ],
        compiler_params=pltpu.CompilerParams(
            dimension_semantics=("arbitrary", "arbitrary"), vmem_limit_bytes=VMEM_LIMIT),
        name="banded_attention",
    )(sinks, qkv_a, qkv_a, qkv_b0, qkv_b0, qkv_b1, qkv_b1, qkv_b2)


def _layer_norm(z, g, b):
    mu = jnp.mean(z, axis=-1, keepdims=True)
    zc = z - mu
    var = jnp.mean(zc * zc, axis=-1, keepdims=True)
    return zc * lax.rsqrt(var + LN_EPS) * g + b


def _merge_kernel(x_ref, oa_ref, ob_ref, wg_ref, wa_ref, wb_ref, wo_ref, wr_ref, br_ref, g_ref, b_ref,
                  h_ref, lp_ref, gw_ref, cnt_ref, logits_scr):
    i = pl.program_id(0)

    @pl.when(i == 0)
    def _():
        logits_scr[...] = jnp.zeros_like(logits_scr)

    logits = logits_scr[(i + 1) % 2]

    x = x_ref[...]
    xb = x.astype(BF16)
    gates = jax.nn.sigmoid(jnp.dot(xb, wg_ref[...], preferred_element_type=F32))
    ya = jnp.dot(oa_ref[...], wa_ref[...], preferred_element_type=F32)
    yb = jnp.dot(ob_ref[...], wb_ref[...], preferred_element_type=F32)
    merged = gates[:, :D_MODEL] * ya + gates[:, D_MODEL:] * yb
    mix = jnp.dot(merged.astype(BF16), wo_ref[...], preferred_element_type=F32)
    h = _layer_norm(DEEPNORM_ALPHA * x + mix, g_ref[...], b_ref[...])
    h_ref[...] = h
    logits_scr[i % 2] = jnp.dot(h.astype(BF16), wr_ref[...], preferred_element_type=F32) + br_ref[...]

    tm = logits.shape[0]
    lane = lax.broadcasted_iota(I32, (tm, ROUTER_PAD), 1)
    lane_f = lane.astype(F32)
    v_out = jnp.full((tm, ROUTER_PAD), NEG, F32)
    onehots = []
    lg = logits
    for k in range(TOP_K):
        m = jnp.max(lg, axis=-1, keepdims=True)
        idx = jnp.min(jnp.where(lg == m, lane_f, float(ROUTER_PAD)), axis=-1, keepdims=True)
        chosen = lane_f == idx
        onehots.append(chosen.astype(F32))
        v_out = jnp.where(lane == k, m, v_out)
        lg = jnp.where(chosen, 3.0 * NEG, lg)
    vmax = jnp.max(v_out, axis=-1, keepdims=True)
    ex = jnp.exp(v_out - vmax)
    gw_ref[...] = jnp.transpose(ex / jnp.sum(ex, axis=-1, keepdims=True))[:TOP_K]

    routed = onehots[0] + onehots[1] + onehots[2] + onehots[3]
    r_i = lax.broadcasted_iota(I32, (tm, tm), 0)
    c_i = lax.broadcasted_iota(I32, (tm, tm), 1)
    before = jnp.dot((c_i < r_i).astype(BF16), routed.astype(BF16), preferred_element_type=F32)
    cnt = jnp.sum(routed, axis=0, keepdims=True)
    runs = jnp.ceil(cnt * (1.0 / RUN_ALIGN))
    e_r = lax.broadcasted_iota(I32, (ROUTER_PAD, ROUTER_PAD), 0)
    e_c = lax.broadcasted_iota(I32, (ROUTER_PAD, ROUTER_PAD), 1)
    off = jnp.dot(jnp.broadcast_to(runs, (SUBLANES, ROUTER_PAD)).astype(BF16), (e_r < e_c).astype(BF16),
                  preferred_element_type=F32)[0:1] * float(RUN_ALIGN)
    base = before + off
    lp = jnp.zeros((tm, ROUTER_PAD), F32)
    for k in range(TOP_K):
        lp = jnp.where(lane == k, jnp.sum(onehots[k] * base, axis=-1, keepdims=True), lp)
    lp_ref[...] = jnp.transpose(lp)[:TOP_K].astype(I32)
    cnt_ref[...] = cnt


def _merge(x2, oa2, ob2, wg, wa, wb, wo, wr, br, ln_g, ln_b):
    t, d = x2.shape
    tm = ROW_TILE
    nt = t // tm
    full = lambda shape: pl.BlockSpec(shape, lambda i: (0, 0))
    dense = lambda i: (jnp.minimum(i, nt - 1), 0)
    routed = lambda i: (jnp.maximum(i - 1, 0), 0, 0)
    return pl.pallas_call(
        _merge_kernel,
        grid=(nt + 1,),
        in_specs=[
            pl.BlockSpec((tm, d), dense),
            pl.BlockSpec((tm, A_Q_W), dense),
            pl.BlockSpec((tm, B_GROUP_W), dense),
            full((d, 2 * d)), full((A_Q_W, d)), full((B_GROUP_W, d)), full((d, d)),
            full((d, ROUTER_PAD)), full((1, ROUTER_PAD)), full((1, d)), full((1, d)),
        ],
        out_specs=[
            pl.BlockSpec((tm, d), dense),
            pl.BlockSpec((None, TOP_K, tm), routed),
            pl.BlockSpec((None, TOP_K, tm), routed),
            pl.BlockSpec((None, 1, ROUTER_PAD), routed),
        ],
        out_shape=[
            jax.ShapeDtypeStruct((t, d), F32),
            jax.ShapeDtypeStruct((nt, TOP_K, tm), I32),
            jax.ShapeDtypeStruct((nt, TOP_K, tm), F32),
            jax.ShapeDtypeStruct((nt, 1, ROUTER_PAD), F32),
        ],
        scratch_shapes=[pltpu.VMEM((2, tm, ROUTER_PAD), F32)],
        compiler_params=pltpu.CompilerParams(
            dimension_semantics=("arbitrary",), vmem_limit_bytes=VMEM_LIMIT),
        name="merge_ln_router",
    )(x2, oa2, ob2, wg, wa, wb, wo, wr, br, ln_g, ln_b)


def _pack_rows(v, already_bf16=False):
    vb = v if already_bf16 else v.astype(BF16).astype(F32)
    lo = lax.bitcast_convert_type(vb[:, :HALF_W], WORD) >> 16
    hi = lax.bitcast_convert_type(vb[:, HALF_W:], WORD) & jnp.uint32(0xFFFF0000)
    return lo | hi


def _unpack_rows(w):
    lo = lax.bitcast_convert_type(w << 16, F32)
    hi = lax.bitcast_convert_type(w & jnp.uint32(0xFFFF0000), F32)
    return jnp.concatenate([lo, hi], axis=1).astype(BF16)


def _run_copy(tile, e, meta, vmem_rows, hbm_rows, sem, to_hbm):
    run_start_ref, n8_ref, off_ref = meta
    idx = tile * N_EXPERTS + e
    n = pl.multiple_of(n8_ref[idx], RUN_ALIGN)
    local = vmem_rows.at[pl.ds(pl.multiple_of(off_ref[idx], RUN_ALIGN), n)]
    remote = hbm_rows.at[pl.ds(pl.multiple_of(run_start_ref[idx], RUN_ALIGN), n)]
    return (pltpu.make_async_copy(local, remote, sem) if to_hbm
            else pltpu.make_async_copy(remote, local, sem)), n


def _start_runs(tile, meta, vmem_rows, hbm_rows, sem, to_hbm):
    def body(e, carry):
        copy, n = _run_copy(tile, e, meta, vmem_rows, hbm_rows, sem, to_hbm)

        @pl.when(n > 0)
        def _():
            copy.start()
        return carry
    lax.fori_loop(0, N_EXPERTS, body, 0, unroll=4)


def _wait_runs(tile, meta, vmem_rows, hbm_rows, sem, to_hbm):
    _, n8_ref, off_ref = meta
    last = tile * N_EXPERTS + N_EXPERTS - 1
    total = pl.multiple_of(off_ref[last] + n8_ref[last], RUN_ALIGN)
    local = vmem_rows.at[pl.ds(0, total)]
    remote = hbm_rows.at[pl.ds(0, total)]
    (pltpu.make_async_copy(local, remote, sem) if to_hbm else pltpu.make_async_copy(remote, local, sem)).wait()


def _dispatch_kernel(run_start_ref, n8_ref, off_ref, tail_start_ref, tail_len_ref, nused_ref,
                     h_ref, lpt_ref, x_hbm, buf, zbuf, sem, zsem):
    i = pl.program_id(0)
    nt = pl.num_programs(0)
    slot = i % 2
    meta = (run_start_ref, n8_ref, off_ref)

    @pl.when(i >= 2)
    def _():
        _wait_runs(i - 2, meta, buf.at[slot], x_hbm, sem.at[slot], True)

    hb = h_ref[...].astype(BF16)
    lp16 = lpt_ref[...].astype(jnp.int16)
    one = jnp.ones((PERM_CHUNK, ROW_TILE), BF16)
    for c in range(PERM_ROWS // PERM_CHUNK):
        r0 = c * PERM_CHUNK
        rows = (lax.broadcasted_iota(I32, (PERM_CHUNK, ROW_TILE), 0) + r0).astype(jnp.int16)
        perm = jnp.zeros((PERM_CHUNK, ROW_TILE), BF16)
        for k in range(TOP_K):
            perm = jnp.where(rows == lp16[k:k + 1, :], one, perm)
        buf[slot, r0:r0 + PERM_CHUNK, :] = _pack_rows(jnp.dot(perm, hb, preferred_element_type=F32),
                                                      already_bf16=True)

    _start_runs(i, meta, buf.at[slot], x_hbm, sem.at[slot], True)

    @pl.when(i == nt - 1)
    def _():
        zbuf[...] = jnp.zeros_like(zbuf)

        def tail_copy(e):
            n = pl.multiple_of(tail_len_ref[e], RUN_ALIGN)
            dst = x_hbm.at[pl.ds(pl.multiple_of(tail_start_ref[e], RUN_ALIGN), n)]
            return pltpu.make_async_copy(zbuf.at[pl.ds(0, n)], dst, zsem), n

        def start_tail(e, carry):
            copy, n = tail_copy(e)

            @pl.when(n > 0)
            def _():
                copy.start()
            return carry

        def wait_tail(e, carry):
            copy, n = tail_copy(e)

            @pl.when(n > 0)
            def _():
                copy.wait()
            return carry

        lax.fori_loop(0, N_EXPERTS, start_tail, 0)
        lax.fori_loop(0, N_EXPERTS, wait_tail, 0)

        def spare_copy(blk_i):
            dst = x_hbm.at[pl.ds(pl.multiple_of(blk_i * MOE_BLOCK, MOE_BLOCK), MOE_BLOCK)]
            return pltpu.make_async_copy(zbuf, dst, zsem)

        def start_spare(blk_i, carry):
            spare_copy(blk_i).start()
            return carry

        def wait_spare(blk_i, carry):
            spare_copy(blk_i).wait()
            return carry

        n_blocks = x_hbm.shape[0] // MOE_BLOCK
        lax.fori_loop(nused_ref[0], n_blocks, start_spare, 0)
        lax.fori_loop(nused_ref[0], n_blocks, wait_spare, 0)

        @pl.when(nt >= 2)
        def _():
            _wait_runs(i - 1, meta, buf.at[1 - slot], x_hbm, sem.at[1 - slot], True)
        _wait_runs(i, meta, buf.at[slot], x_hbm, sem.at[slot], True)


def _dispatch(meta, tails, n_used, h2, lp_t, n_rows):
    t, d = h2.shape
    nt = t // ROW_TILE
    grid_spec = pltpu.PrefetchScalarGridSpec(
        num_scalar_prefetch=6,
        grid=(nt,),
        in_specs=[
            pl.BlockSpec((ROW_TILE, d), lambda i, *_: (i, 0)),
            pl.BlockSpec((None, TOP_K, ROW_TILE), lambda i, *_: (i, 0, 0)),
        ],
        out_specs=pl.BlockSpec(memory_space=pl.ANY),
        scratch_shapes=[
            pltpu.VMEM((2, PERM_ROWS, HALF_W), WORD),
            pltpu.VMEM((MOE_BLOCK, HALF_W), WORD),
            pltpu.SemaphoreType.DMA((2,)),
            pltpu.SemaphoreType.DMA(()),
        ],
    )
    return pl.pallas_call(
        _dispatch_kernel,
        grid_spec=grid_spec,
        out_shape=jax.ShapeDtypeStruct((n_rows, HALF_W), WORD),
        compiler_params=pltpu.CompilerParams(
            dimension_semantics=("arbitrary",), vmem_limit_bytes=VMEM_LIMIT, has_side_effects=True),
        name="dispatch",
    )(*meta, *tails, n_used, h2, lp_t)


def _ffn_kernel(blk_e_ref, nused_ref, x_ref, wgu_ref, bgu_ref, wdn_ref, bdn_ref, y_ref, wgu_bf, wdn_bf):
    i = pl.program_id(0)
    used = i < nused_ref[0]
    new_expert = jnp.logical_or(i == 0, blk_e_ref[i] != blk_e_ref[jnp.maximum(i - 1, 0)])

    @pl.when(jnp.logical_and(used, new_expert))
    def _():
        wgu_bf[...] = wgu_ref[...].astype(BF16)
        wdn_bf[...] = wdn_ref[...].astype(BF16)

    @pl.when(used)
    def _():
        xb = _unpack_rows(x_ref[...])
        acc = jnp.zeros((MOE_BLOCK, D_MODEL), F32)
        for c in range(D_EXPERT // FFN_CHUNK):
            lo = c * FFN_CHUNK
            gate = jnp.dot(xb, wgu_bf[:, lo:lo + FFN_CHUNK], preferred_element_type=F32)
            gate = gate + bgu_ref[:, lo:lo + FFN_CHUNK]
            up = jnp.dot(xb, wgu_bf[:, D_EXPERT + lo:D_EXPERT + lo + FFN_CHUNK], preferred_element_type=F32)
            up = up + bgu_ref[:, D_EXPERT + lo:D_EXPERT + lo + FFN_CHUNK]
            gate = jnp.minimum(gate, SWIGLU_LIMIT)
            up = jnp.clip(up, -SWIGLU_LIMIT, SWIGLU_LIMIT)
            hmid = (up + 1.0) * gate * jax.nn.sigmoid(SWIGLU_ALPHA * gate)
            acc = acc + jnp.dot(hmid.astype(BF16), wdn_bf[lo:lo + FFN_CHUNK, :], preferred_element_type=F32)
        y_ref[...] = _pack_rows(acc + bdn_ref[...])

    @pl.when(i >= nused_ref[0])
    def _():
        y_ref[...] = jnp.zeros_like(y_ref)


def _expert_ffn(blk_e, n_used, x_rows, wgu, bgu, wdn, bdn):
    nblk = x_rows.shape[0] // MOE_BLOCK
    d = D_MODEL
    grid_spec = pltpu.PrefetchScalarGridSpec(
        num_scalar_prefetch=2,
        grid=(nblk,),
        in_specs=[
            pl.BlockSpec((MOE_BLOCK, HALF_W), lambda i, be, nu: (jnp.minimum(i, nu[0] - 1), 0)),
            pl.BlockSpec((None, d, 2 * D_EXPERT), lambda i, be, nu: (be[i], 0, 0)),
            pl.BlockSpec((None, 1, 2 * D_EXPERT), lambda i, be, nu: (be[i], 0, 0)),
            pl.BlockSpec((None, D_EXPERT, d), lambda i, be, nu: (be[i], 0, 0)),
            pl.BlockSpec((None, 1, d), lambda i, be, nu: (be[i], 0, 0)),
        ],
        out_specs=pl.BlockSpec((MOE_BLOCK, HALF_W), lambda i, be, nu: (i, 0)),
        scratch_shapes=[pltpu.VMEM((d, 2 * D_EXPERT), BF16), pltpu.VMEM((D_EXPERT, d), BF16)],
    )
    return pl.pallas_call(
        _ffn_kernel,
        grid_spec=grid_spec,
        out_shape=jax.ShapeDtypeStruct((nblk * MOE_BLOCK, HALF_W), WORD),
        compiler_params=pltpu.CompilerParams(
            dimension_semantics=("arbitrary",), vmem_limit_bytes=FFN_VMEM_LIMIT),
        name="expert_ffn",
    )(blk_e, n_used, x_rows, wgu, bgu, wdn, bdn)


def _combine_kernel(run_start_ref, n8_ref, off_ref, y_hbm, h_ref, lpt_ref, gwt_ref, g_ref, b_ref,
                    out_ref, ybuf, wperm_scr, ysel_scr, sem):
    i = pl.program_id(0)
    nt = pl.num_programs(0)
    slot = i % 2
    meta = (run_start_ref, n8_ref, off_ref)

    @pl.when(i == 0)
    def _():
        ybuf[...] = jnp.zeros_like(ybuf)
        _start_runs(0, meta, ybuf.at[0], y_hbm, sem.at[0], False)

    @pl.when(i + 1 < nt)
    def _():
        _start_runs(i + 1, meta, ybuf.at[1 - slot], y_hbm, sem.at[1 - slot], False)

    _wait_runs(i, meta, ybuf.at[slot], y_hbm, sem.at[slot], False)

    lp16 = lpt_ref[...].astype(jnp.int16)
    gw16 = gwt_ref[...].astype(BF16)
    for c in range(PERM_ROWS // PERM_CHUNK):
        r0 = c * PERM_CHUNK
        rows = (lax.broadcasted_iota(I32, (PERM_CHUNK, ROW_TILE), 0) + r0).astype(jnp.int16)
        wperm = jnp.zeros((PERM_CHUNK, ROW_TILE), BF16)
        for k in range(TOP_K):
            wperm = jnp.where(rows == lp16[k:k + 1, :], gw16[k:k + 1, :], wperm)
        wperm_scr[r0:r0 + PERM_CHUNK, :] = wperm
        ysel_scr[r0:r0 + PERM_CHUNK, :] = _unpack_rows(ybuf[slot, r0:r0 + PERM_CHUNK, :])
    half = ROW_TILE // 2
    for r in range(2):
        tok = slice(r * half, (r + 1) * half)
        ffn = lax.dot_general(wperm_scr[:, tok], ysel_scr[...], (((0,), (0,)), ((), ())),
                              preferred_element_type=F32)
        out_ref[tok, :] = _layer_norm(DEEPNORM_ALPHA * h_ref[tok, :] + ffn, g_ref[...], b_ref[...])


def _combine(meta, y_rows, h2, lp_t, gw_t, ln_g, ln_b):
    t, d = h2.shape
    nt = t // ROW_TILE
    grid_spec = pltpu.PrefetchScalarGridSpec(
        num_scalar_prefetch=3,
        grid=(nt,),
        in_specs=[
            pl.BlockSpec(memory_space=pl.ANY),
            pl.BlockSpec((ROW_TILE, d), lambda i, *_: (i, 0)),
            pl.BlockSpec((None, TOP_K, ROW_TILE), lambda i, *_: (i, 0, 0)),
            pl.BlockSpec((None, TOP_K, ROW_TILE), lambda i, *_: (i, 0, 0)),
            pl.BlockSpec((1, d), lambda i, *_: (0, 0)),
            pl.BlockSpec((1, d), lambda i, *_: (0, 0)),
        ],
        out_specs=pl.BlockSpec((ROW_TILE, d), lambda i, *_: (i, 0)),
        scratch_shapes=[pltpu.VMEM((2, PERM_ROWS, HALF_W), WORD), pltpu.VMEM((PERM_ROWS, ROW_TILE), BF16),
                        pltpu.VMEM((PERM_ROWS, d), BF16), pltpu.SemaphoreType.DMA((2,))],
    )
    return pl.pallas_call(
        _combine_kernel,
        grid_spec=grid_spec,
        out_shape=jax.ShapeDtypeStruct((t, d), F32),
        compiler_params=pltpu.CompilerParams(
            dimension_semantics=("arbitrary",), vmem_limit_bytes=VMEM_LIMIT),
        name="combine_ln",
    )(*meta, y_rows, h2, lp_t, gw_t, ln_g, ln_b)


def _max_blocks(t):
    nt = t // ROW_TILE
    rows = t * TOP_K + nt * N_EXPERTS * (RUN_ALIGN - 1) + N_EXPERTS * (MOE_BLOCK - RUN_ALIGN)
    return -(-rows // MOE_BLOCK)


def _layout(counts, nblk):
    n8 = (counts + RUN_ALIGN - 1) // RUN_ALIGN * RUN_ALIGN
    total = jnp.sum(n8, axis=0)
    padded = (total + MOE_BLOCK - 1) // MOE_BLOCK * MOE_BLOCK
    e_end = jnp.cumsum(padded)
    e_start = e_end - padded
    run_start = e_start[None, :] + jnp.cumsum(n8, axis=0) - n8
    off = jnp.cumsum(n8, axis=1) - n8
    blk_row0 = jnp.arange(nblk, dtype=I32) * MOE_BLOCK
    blk_e = jnp.minimum(jnp.sum((e_end[None, :] <= blk_row0[:, None]).astype(I32), axis=1), N_EXPERTS - 1)
    n_used = (e_end[-1] // MOE_BLOCK).reshape(1)
    meta = tuple(a.reshape(-1).astype(I32) for a in (run_start, n8, off))
    tails = ((e_start + total).astype(I32), (padded - total).astype(I32))
    return meta, tails, blk_e.astype(I32), n_used.astype(I32)


def kernel(x, w_in, attn_sinks, w_branch_a, w_branch_b, w_out, ln1_g, ln1_b, w_router, b_router,
           w_gate_up, b_gate_up, w_down, b_down, ln2_g, ln2_b):
    depth = w_in.shape[0]
    b, s, d = x.shape
    t = b * s
    nt = t // ROW_TILE

    pos = jnp.arange(s, dtype=F32)
    inv_freq = ROPE_THETA ** (-jnp.arange(0, HEAD_DIM, 2, dtype=F32) / HEAD_DIM)
    ang = pos[:, None] * inv_freq[None, :]
    cos32, sin32 = jnp.cos(ang), jnp.sin(ang)
    cos_t = jnp.tile(cos32, (1, LANES // (HEAD_DIM // 2)))
    sin_t = jnp.tile(jnp.concatenate([-sin32, sin32], axis=1), (1, LANES // HEAD_DIM))

    h = x
    for l in range(depth):
        wi = w_in[l]
        qscale = HEAD_DIM ** -0.5 * LOG2E
        cols = [wi[:, :A_Q_W] * qscale, wi[:, OFF_AK:OFF_BQ]]
        for g in range(len(B_GROUPS)):
            cols += [wi[:, OFF_BQ + g * B_GROUP_W:OFF_BQ + (g + 1) * B_GROUP_W] * qscale,
                     wi[:, OFF_BK + g * B_GROUP_W:OFF_BK + (g + 1) * B_GROUP_W],
                     wi[:, OFF_BV + g * B_GROUP_W:OFF_BV + (g + 1) * B_GROUP_W]]
        w_qkv = jnp.concatenate(cols, axis=1).astype(BF16)

        qkv_a, qkv_b0, qkv_b1, qkv_b2 = _qkv_rope(h, w_qkv, cos_t, sin_t)
        oa, ob = _attention(attn_sinks[l].astype(F32) * LOG2E, qkv_a, qkv_b0, qkv_b1, qkv_b2)

        wr = jnp.zeros((d, ROUTER_PAD), F32).at[:, :N_EXPERTS].set(w_router[l]).astype(BF16)
        br = jnp.full((1, ROUTER_PAD), NEG, F32).at[0, :N_EXPERTS].set(b_router[l].astype(F32))
        h1, lp_t, gw_t, counts = _merge(
            h.reshape(t, d), oa.reshape(t, A_Q_W), ob.reshape(t, B_GROUP_W),
            wi[:, OFF_G:].astype(BF16), w_branch_a[l].astype(BF16), w_branch_b[l].astype(BF16),
            w_out[l].astype(BF16), wr, br, ln1_g[l].reshape(1, d), ln1_b[l].reshape(1, d))

        nblk = _max_blocks(t)
        meta, tails, blk_e, n_used = _layout(counts.reshape(nt, ROUTER_PAD)[:, :N_EXPERTS].astype(I32), nblk)
        x_rows = _dispatch(meta, tails, n_used, h1, lp_t, nblk * MOE_BLOCK)
        y_rows = _expert_ffn(
            blk_e, n_used, x_rows,
            w_gate_up[l], b_gate_up[l].reshape(N_EXPERTS, 1, 2 * D_EXPERT),
            w_down[l], b_down[l].reshape(N_EXPERTS, 1, d))
        h = _combine(meta, y_rows, h1, lp_t, gw_t, ln2_g[l].reshape(1, d), ln2_b[l].reshape(1, d)).reshape(b, s, d)
    return h
```

```python
import jax
import jax.numpy as jnp
from jax import lax
from jax.experimental import pallas as pl
from jax.experimental.pallas import tpu as pltpu

F32 = jnp.float32
BF16 = jnp.bfloat16
I32 = jnp.int32
WORD = jnp.uint32

D_MODEL = 1024
HEAD_DIM = 64
ROPE_THETA = 10000.0
A_Q_W = 1024
A_KV_W = 256
A_KV_HEADS = 4
A_GQA = 4
A_WINDOW = 128
B_GROUPS = ((128, 1), (512, 4), (2048, 16))
B_GROUP_W = 256
B_HEADS = 4
B_W = 768
OFF_AK = A_Q_W
OFF_AV = OFF_AK + A_KV_W
OFF_BQ = OFF_AV + A_KV_W
OFF_BK = OFF_BQ + B_W
OFF_BV = OFF_BK + B_W
OFF_G = OFF_BV + B_W
N_EXPERTS = 32
TOP_K = 4
D_EXPERT = 1024
SWIGLU_LIMIT = 7.0
SWIGLU_ALPHA = 1.702
LN_EPS = 1e-5
DEEPNORM_ALPHA = 2.0 ** 0.25
LOG2E = 1.4426950408889634

LANES = 128
SUBLANES = 8
ATTN_BLOCK = 128
SEQ_TILE = 512
ROW_TILE = 512
MOE_BLOCK = 1024
FFN_CHUNK = 512
RUN_ALIGN = SUBLANES
PERM_ROWS = ROW_TILE * TOP_K + N_EXPERTS * RUN_ALIGN
PERM_CHUNK = 256
A_SEG_W = A_Q_W + 2 * A_KV_W
B_SEG_W = 3 * B_GROUP_W
QKV_W = A_SEG_W + 3 * B_SEG_W
HALF_W = D_MODEL // 2
ROUTER_PAD = LANES
NEG = -1e30
VMEM_LIMIT = 48 * 1024 * 1024
FFN_VMEM_LIMIT = 56 * 1024 * 1024


def _qkv_rope_kernel(x_ref, w_ref, cos_ref, sin_ref, a_ref, b0_ref, b1_ref, b2_ref, scr_ref):
    tm = x_ref.shape[0]
    xb = x_ref[...].astype(BF16)
    cos = cos_ref[...]
    sin = sin_ref[...]
    lane = lax.broadcasted_iota(I32, (tm, LANES), 1)
    first_half = (lane % HEAD_DIM) < (HEAD_DIM // 2)

    def rope(a):
        partner = jnp.where(first_half, pltpu.roll(a, LANES - 32, 1), pltpu.roll(a, 32, 1))
        return a * cos + partner * sin

    def segment(col0, width, n_rope_chunks):
        acc = jnp.dot(xb, w_ref[:, col0:col0 + width], preferred_element_type=F32)
        pieces = []
        for c in range(width // LANES):
            a = acc[:, c * LANES:(c + 1) * LANES]
            pieces.append(rope(a) if c < n_rope_chunks else a)
        return pieces

    a_ref[...] = jnp.concatenate(segment(0, A_SEG_W, (A_Q_W + A_KV_W) // LANES), axis=1).astype(BF16)
    b0_ref[...] = jnp.concatenate(segment(A_SEG_W, B_SEG_W, 2 * B_GROUP_W // LANES), axis=1).astype(BF16)
    n_chunks = B_SEG_W // LANES
    for out_ref, col0, dil in ((b1_ref, A_SEG_W + B_SEG_W, 4), (b2_ref, A_SEG_W + 2 * B_SEG_W, 16)):
        for k, piece in enumerate(segment(col0, B_SEG_W, 2 * B_GROUP_W // LANES)):
            scr_ref[k] = piece
        for c in range(dil):
            rows = [scr_ref[k, pl.ds(c, tm // dil, stride=dil), :] for k in range(n_chunks)]
            out_ref[c] = jnp.concatenate(rows, axis=1).astype(BF16)


def _qkv_rope(x, w_qkv, cos_t, sin_t):
    b, s, d = x.shape
    tm = SEQ_TILE
    grid = (b, s // tm)
    return pl.pallas_call(
        _qkv_rope_kernel,
        grid=grid,
        in_specs=[
            pl.BlockSpec((None, tm, d), lambda bi, si: (bi, si, 0)),
            pl.BlockSpec((d, QKV_W), lambda bi, si: (0, 0)),
            pl.BlockSpec((tm, LANES), lambda bi, si: (si, 0)),
            pl.BlockSpec((tm, LANES), lambda bi, si: (si, 0)),
        ],
        out_specs=[
            pl.BlockSpec((None, tm, A_SEG_W), lambda bi, si: (bi, si, 0)),
            pl.BlockSpec((None, tm, B_SEG_W), lambda bi, si: (bi, si, 0)),
            pl.BlockSpec((None, 4, tm // 4, B_SEG_W), lambda bi, si: (bi, 0, si, 0)),
            pl.BlockSpec((None, 16, tm // 16, B_SEG_W), lambda bi, si: (bi, 0, si, 0)),
        ],
        out_shape=[
            jax.ShapeDtypeStruct((b, s, A_SEG_W), BF16),
            jax.ShapeDtypeStruct((b, s, B_SEG_W), BF16),
            jax.ShapeDtypeStruct((b, 4, s // 4, B_SEG_W), BF16),
            jax.ShapeDtypeStruct((b, 16, s // 16, B_SEG_W), BF16),
        ],
        scratch_shapes=[pltpu.VMEM((B_SEG_W // LANES, tm, LANES), F32)],
        compiler_params=pltpu.CompilerParams(
            dimension_semantics=("arbitrary", "arbitrary"), vmem_limit_bytes=VMEM_LIMIT),
        name="qkv_rope",
    )(x, w_qkv, cos_t, sin_t)


def _stacked_heads_unit(q, kmat, vmat, bias, sink_slot=False):
    blk = q.shape[0]
    group = lax.broadcasted_iota(I32, (1, B_GROUP_W), 1) // HEAD_DIM
    zero = jnp.zeros_like(q)
    qs = jnp.concatenate([jnp.where(group == h, q, zero) for h in range(B_HEADS)], axis=0)
    if sink_slot:
        tile = 2 * SUBLANES
        not_slot = lax.broadcasted_iota(I32, (tile, B_GROUP_W), 0) > 0
        kmat = jnp.concatenate([jnp.where(not_slot, kmat[:tile], jnp.zeros_like(kmat[:tile])), kmat[tile:]], axis=0)
        vmat = jnp.concatenate([jnp.where(not_slot, vmat[:tile], jnp.zeros_like(vmat[:tile])), vmat[tile:]], axis=0)
    s = lax.dot_general(qs, kmat, (((1,), (1,)), ((), ())), preferred_element_type=F32) + bias
    m = jnp.max(s, axis=-1, keepdims=True)
    p = jnp.exp2(s - m)
    l = jnp.sum(p, axis=-1, keepdims=True)
    o = jnp.dot(p.astype(BF16), vmat, preferred_element_type=F32)
    out = jnp.zeros((blk, B_GROUP_W), F32)
    l_b = jnp.ones((blk, B_GROUP_W), F32)
    m_b = jnp.zeros((blk, B_GROUP_W), F32)
    for h in range(B_HEADS):
        out = jnp.where(group == h, o[h * blk:(h + 1) * blk], out)
        l_b = jnp.where(group == h, l[h * blk:(h + 1) * blk], l_b)
        m_b = jnp.where(group == h, m[h * blk:(h + 1) * blk], m_b)
    return out * (1.0 / l_b), m_b + jnp.log2(l_b)


def _store_lane_chunks(ref, lead, rows, x):
    for c in range(x.shape[1] // LANES):
        ref[(*lead, c, rows, slice(None))] = x[:, c * LANES:(c + 1) * LANES]


def _attention_kernel(sink_ref, a_cur, a_prev, b0_cur, b0_prev, b1_cur, b1_prev, b2_ref,
                      oa_ref, ob_ref, kexp_scr, vexp_scr, kvb_scr, o_scr, lse_scr, o2_scr, lse2_scr,
                      band_scr, sinkband_scr, causal_scr):
    j = pl.program_id(1)
    blk = ATTN_BLOCK
    nqb = SEQ_TILE // blk
    stack = B_HEADS * blk

    @pl.when(jnp.logical_and(pl.program_id(0) == 0, j == 0))
    def _():
        row = lax.broadcasted_iota(I32, (stack, 2 * blk), 0) % blk
        col = lax.broadcasted_iota(I32, (stack, 2 * blk), 1)
        row_block = lax.broadcasted_iota(I32, (stack, 2 * blk), 0) // blk
        dist = blk + row - col
        in_band = (dist >= 0) & (dist <= B_GROUPS[0][0] // B_GROUPS[0][1])
        band_scr[0] = jnp.where(in_band, 0.0, NEG)
        band_scr[1] = jnp.where(in_band & (col >= blk), 0.0, NEG)
        in_band = (dist >= 0) & (dist <= A_WINDOW - 1)
        for kvh in range(A_KV_HEADS):
            sink = jnp.zeros((stack, 2 * blk), F32)
            for g in range(A_GQA):
                sink = jnp.where(row_block == g, sink_ref[kvh * A_GQA + g], sink)
            sinkband_scr[2 * kvh] = jnp.where(col == 0, sink, jnp.where(in_band, 0.0, NEG))
            sinkband_scr[2 * kvh + 1] = jnp.where(col == 0, sink, jnp.where(in_band & (col >= blk), 0.0, NEG))
        r1 = lax.broadcasted_iota(I32, (stack, blk), 0) % blk
        c1 = lax.broadcasted_iota(I32, (stack, blk), 1)
        causal_scr[...] = jnp.where(c1 <= r1, 0.0, NEG)

    @pl.when(j == 0)
    def _():
        def class_body(c, carry):
            t = b2_ref[c]
            out, lse_b = _stacked_heads_unit(t[:, :B_GROUP_W], t[:, B_GROUP_W:2 * B_GROUP_W],
                                             t[:, 2 * B_GROUP_W:], causal_scr[...])
            rows = pl.ds(c, blk, stride=16)
            _store_lane_chunks(o2_scr, (), rows, out)
            _store_lane_chunks(lse2_scr, (), rows, lse_b)
            return carry

        lax.fori_loop(0, 16, class_body, 0, unroll=4)

    for src, r0, r1 in ((a_prev, 0, blk), (a_cur, blk, blk + SEQ_TILE)):
        for dst, col0 in ((kexp_scr, A_Q_W), (vexp_scr, A_Q_W + A_KV_W)):
            heads = src[:, col0:col0 + A_KV_W]
            for kvh in range(A_KV_HEADS):
                head = heads[:, kvh * HEAD_DIM:(kvh + 1) * HEAD_DIM]
                dst[r0:r1, kvh * B_GROUP_W:(kvh + 1) * B_GROUP_W] = jnp.concatenate([head] * A_GQA, axis=1)
    kvb_scr[0:blk, :] = b0_prev[:, B_GROUP_W:]
    kvb_scr[blk:, :] = b0_cur[:, B_GROUP_W:]

    def qblock_body(qb, carry):
        r0 = pl.multiple_of(qb * blk, blk)
        first = jnp.logical_and(j == 0, qb == 0).astype(I32)
        for kvh in range(A_KV_HEADS):
            lanes = slice(kvh * B_GROUP_W, (kvh + 1) * B_GROUP_W)
            out, _ = _stacked_heads_unit(a_cur[pl.ds(r0, blk), lanes], kexp_scr[pl.ds(r0, 2 * blk), lanes],
                                         vexp_scr[pl.ds(r0, 2 * blk), lanes], sinkband_scr[2 * kvh + first],
                                         sink_slot=True)
            oa_ref[pl.ds(r0, blk), lanes] = out.astype(oa_ref.dtype)
        out, lse_b = _stacked_heads_unit(b0_cur[pl.ds(r0, blk), 0:B_GROUP_W],
                                         kvb_scr[pl.ds(r0, 2 * blk), 0:B_GROUP_W],
                                         kvb_scr[pl.ds(r0, 2 * blk), B_GROUP_W:], band_scr[first])
        _store_lane_chunks(o_scr, (0,), pl.ds(r0, blk), out)
        _store_lane_chunks(lse_scr, (0,), pl.ds(r0, blk), lse_b)
        return carry

    lax.fori_loop(0, nqb, qblock_body, 0, unroll=True)

    first_class_block = (j == 0).astype(I32)

    def class4_body(c, carry):
        cur = b1_cur[c]
        prev = b1_prev[c]
        kv = jnp.concatenate([prev[:, B_GROUP_W:], cur[:, B_GROUP_W:]], axis=0)
        out, lse_b = _stacked_heads_unit(cur[:, :B_GROUP_W], kv[:, :B_GROUP_W], kv[:, B_GROUP_W:],
                                         band_scr[first_class_block])
        rows = pl.ds(c, blk, stride=4)
        _store_lane_chunks(o_scr, (1,), rows, out)
        _store_lane_chunks(lse_scr, (1,), rows, lse_b)
        return carry

    lax.fori_loop(0, 4, class4_body, 0, unroll=True)

    t0 = pl.multiple_of(j * SEQ_TILE, SEQ_TILE)
    for c in range(B_GROUP_W // LANES):
        l0, l1, l2 = lse_scr[0, c], lse_scr[1, c], lse2_scr[c, pl.ds(t0, SEQ_TILE), :]
        mx = jnp.maximum(jnp.maximum(l0, l1), l2)
        w0, w1, w2 = jnp.exp2(l0 - mx), jnp.exp2(l1 - mx), jnp.exp2(l2 - mx)
        num = w0 * o_scr[0, c] + w1 * o_scr[1, c] + w2 * o2_scr[c, pl.ds(t0, SEQ_TILE), :]
        ob_ref[:, c * LANES:(c + 1) * LANES] = (num / (w0 + w1 + w2)).astype(ob_ref.dtype)


def _attention(sinks, qkv_a, qkv_b0, qkv_b1, qkv_b2):
    b, s, _ = qkv_a.shape
    t = SEQ_TILE
    blk = ATTN_BLOCK
    nq = t // blk
    grid = (b, s // t)
    n_chunks = B_GROUP_W // LANES
    return pl.pallas_call(
        _attention_kernel,
        grid=grid,
        in_specs=[
            pl.BlockSpec(memory_space=pltpu.SMEM),
            pl.BlockSpec((None, t, A_SEG_W), lambda bi, j: (bi, j, 0)),
            pl.BlockSpec((None, blk, A_SEG_W), lambda bi, j: (bi, jnp.maximum(j * nq - 1, 0), 0)),
            pl.BlockSpec((None, t, B_SEG_W), lambda bi, j: (bi, j, 0)),
            pl.BlockSpec((None, blk, B_SEG_W), lambda bi, j: (bi, jnp.maximum(j * nq - 1, 0), 0)),
            pl.BlockSpec((None, 4, blk, B_SEG_W), lambda bi, j: (bi, 0, j, 0)),
            pl.BlockSpec((None, 4, blk, B_SEG_W), lambda bi, j: (bi, 0, jnp.maximum(j - 1, 0), 0)),
            pl.BlockSpec((None, 16, blk, B_SEG_W), lambda bi, j: (bi, 0, 0, 0)),
        ],
        out_specs=[
            pl.BlockSpec((None, t, A_Q_W), lambda bi, j: (bi, j, 0)),
            pl.BlockSpec((None, t, B_GROUP_W), lambda bi, j: (bi, j, 0)),
        ],
        out_shape=[
            jax.ShapeDtypeStruct((b, s, A_Q_W), BF16),
            jax.ShapeDtypeStruct((b, s, B_GROUP_W), BF16),
        ],
        scratch_shapes=[
            pltpu.VMEM((blk + t, A_KV_HEADS * B_GROUP_W), BF16),
            pltpu.VMEM((blk + t, A_KV_HEADS * B_GROUP_W), BF16),
            pltpu.VMEM((blk + t, 2 * B_GROUP_W), BF16),
            pltpu.VMEM((2, n_chunks, t, LANES), F32),
            pltpu.VMEM((2, n_chunks, t, LANES), F32),
            pltpu.VMEM((n_chunks, s, LANES), F32),
            pltpu.VMEM((n_chunks, s, LANES), F32),
            pltpu.VMEM((2, B_HEADS * blk, 2 * blk), F32),
            pltpu.VMEM((2 * A_KV_HEADS, B_HEADS * blk, 2 * blk), F32),
            pltpu.VMEM((B_HEADS * blk, blk), F32),
        ],
        compiler_params=pltpu.CompilerParams(
            dimension_semantics=("arbitrary", "arbitrary"), vmem_limit_bytes=VMEM_LIMIT),
        name="banded_attention",
    )(sinks, qkv_a, qkv_a, qkv_b0, qkv_b0, qkv_b1, qkv_b1, qkv_b2)


def _layer_norm(z, g, b):
    mu = jnp.mean(z, axis=-1, keepdims=True)
    zc = z - mu
    var = jnp.mean(zc * zc, axis=-1, keepdims=True)
    return zc * lax.rsqrt(var + LN_EPS) * g + b


def _merge_kernel(x_ref, oa_ref, ob_ref, wg_ref, wa_ref, wb_ref, wo_ref, wr_ref, br_ref, g_ref, b_ref,
                  h_ref, lp_ref, gw_ref, cnt_ref, logits_scr):
    i = pl.program_id(0)

    @pl.when(i == 0)
    def _():
        logits_scr[...] = jnp.zeros_like(logits_scr)

    logits = logits_scr[(i + 1) % 2]

    x = x_ref[...]
    xb = x.astype(BF16)
    gates = jax.nn.sigmoid(jnp.dot(xb, wg_ref[...], preferred_element_type=F32))
    ya = jnp.dot(oa_ref[...], wa_ref[...], preferred_element_type=F32)
    yb = jnp.dot(ob_ref[...], wb_ref[...], preferred_element_type=F32)
    merged = gates[:, :D_MODEL] * ya + gates[:, D_MODEL:] * yb
    mix = jnp.dot(merged.astype(BF16), wo_ref[...], preferred_element_type=F32)
    h = _layer_norm(DEEPNORM_ALPHA * x + mix, g_ref[...], b_ref[...])
    h_ref[...] = h
    logits_scr[i % 2] = jnp.dot(h.astype(BF16), wr_ref[...], preferred_element_type=F32) + br_ref[...]

    tm = logits.shape[0]
    lane = lax.broadcasted_iota(I32, (tm, ROUTER_PAD), 1)
    lane_f = lane.astype(F32)
    v_out = jnp.full((tm, ROUTER_PAD), NEG, F32)
    onehots = []
    lg = logits
    for k in range(TOP_K):
        m = jnp.max(lg, axis=-1, keepdims=True)
        idx = jnp.min(jnp.where(lg == m, lane_f, float(ROUTER_PAD)), axis=-1, keepdims=True)
        chosen = lane_f == idx
        onehots.append(chosen.astype(F32))
        v_out = jnp.where(lane == k, m, v_out)
        lg = jnp.where(chosen, 3.0 * NEG, lg)
    vmax = jnp.max(v_out, axis=-1, keepdims=True)
    ex = jnp.exp(v_out - vmax)
    gw_ref[...] = jnp.transpose(ex / jnp.sum(ex, axis=-1, keepdims=True))[:TOP_K]

    routed = onehots[0] + onehots[1] + onehots[2] + onehots[3]
    r_i = lax.broadcasted_iota(I32, (tm, tm), 0)
    c_i = lax.broadcasted_iota(I32, (tm, tm), 1)
    before = jnp.dot((c_i < r_i).astype(BF16), routed.astype(BF16), preferred_element_type=F32)
    cnt = jnp.sum(routed, axis=0, keepdims=True)
    runs = jnp.ceil(cnt * (1.0 / RUN_ALIGN))
    e_r = lax.broadcasted_iota(I32, (ROUTER_PAD, ROUTER_PAD), 0)
    e_c = lax.broadcasted_iota(I32, (ROUTER_PAD, ROUTER_PAD), 1)
    off = jnp.dot(jnp.broadcast_to(runs, (SUBLANES, ROUTER_PAD)).astype(BF16), (e_r < e_c).astype(BF16),
                  preferred_element_type=F32)[0:1] * float(RUN_ALIGN)
    base = before + off
    lp = jnp.zeros((tm, ROUTER_PAD), F32)
    for k in range(TOP_K):
        lp = jnp.where(lane == k, jnp.sum(onehots[k] * base, axis=-1, keepdims=True), lp)
    lp_ref[...] = jnp.transpose(lp)[:TOP_K].astype(I32)
    cnt_ref[...] = cnt


def _merge(x2, oa2, ob2, wg, wa, wb, wo, wr, br, ln_g, ln_b):
    t, d = x2.shape
    tm = ROW_TILE
    nt = t // tm
    full = lambda shape: pl.BlockSpec(shape, lambda i: (0, 0))
    dense = lambda i: (jnp.minimum(i, nt - 1), 0)
    routed = lambda i: (jnp.maximum(i - 1, 0), 0, 0)
    return pl.pallas_call(
        _merge_kernel,
        grid=(nt + 1,),
        in_specs=[
            pl.BlockSpec((tm, d), dense),
            pl.BlockSpec((tm, A_Q_W), dense),
            pl.BlockSpec((tm, B_GROUP_W), dense),
            full((d, 2 * d)), full((A_Q_W, d)), full((B_GROUP_W, d)), full((d, d)),
            full((d, ROUTER_PAD)), full((1, ROUTER_PAD)), full((1, d)), full((1, d)),
        ],
        out_specs=[
            pl.BlockSpec((tm, d), dense),
            pl.BlockSpec((None, TOP_K, tm), routed),
            pl.BlockSpec((None, TOP_K, tm), routed),
            pl.BlockSpec((None, 1, ROUTER_PAD), routed),
        ],
        out_shape=[
            jax.ShapeDtypeStruct((t, d), F32),
            jax.ShapeDtypeStruct((nt, TOP_K, tm), I32),
            jax.ShapeDtypeStruct((nt, TOP_K, tm), F32),
            jax.ShapeDtypeStruct((nt, 1, ROUTER_PAD), F32),
        ],
        scratch_shapes=[pltpu.VMEM((2, tm, ROUTER_PAD), F32)],
        compiler_params=pltpu.CompilerParams(
            dimension_semantics=("arbitrary",), vmem_limit_bytes=VMEM_LIMIT),
        name="merge_ln_router",
    )(x2, oa2, ob2, wg, wa, wb, wo, wr, br, ln_g, ln_b)


def _pack_rows(v, already_bf16=False):
    vb = v if already_bf16 else v.astype(BF16).astype(F32)
    lo = lax.bitcast_convert_type(vb[:, :HALF_W], WORD) >> 16
    hi = lax.bitcast_convert_type(vb[:, HALF_W:], WORD) & jnp.uint32(0xFFFF0000)
    return lo | hi


def _unpack_rows(w):
    lo = lax.bitcast_convert_type(w << 16, F32)
    hi = lax.bitcast_convert_type(w & jnp.uint32(0xFFFF0000), F32)
    return jnp.concatenate([lo, hi], axis=1).astype(BF16)


def _run_copy(tile, e, meta, vmem_rows, hbm_rows, sem, to_hbm):
    run_start_ref, n8_ref, off_ref = meta
    idx = tile * N_EXPERTS + e
    n = pl.multiple_of(n8_ref[idx], RUN_ALIGN)
    local = vmem_rows.at[pl.ds(pl.multiple_of(off_ref[idx], RUN_ALIGN), n)]
    remote = hbm_rows.at[pl.ds(pl.multiple_of(run_start_ref[idx], RUN_ALIGN), n)]
    return (pltpu.make_async_copy(local, remote, sem) if to_hbm
            else pltpu.make_async_copy(remote, local, sem)), n


def _start_runs(tile, meta, vmem_rows, hbm_rows, sem, to_hbm):
    def body(e, carry):
        copy, n = _run_copy(tile, e, meta, vmem_rows, hbm_rows, sem, to_hbm)

        @pl.when(n > 0)
        def _():
            copy.start()
        return carry
    lax.fori_loop(0, N_EXPERTS, body, 0, unroll=4)


def _wait_runs(tile, meta, vmem_rows, hbm_rows, sem, to_hbm):
    _, n8_ref, off_ref = meta
    last = tile * N_EXPERTS + N_EXPERTS - 1
    total = pl.multiple_of(off_ref[last] + n8_ref[last], RUN_ALIGN)
    local = vmem_rows.at[pl.ds(0, total)]
    remote = hbm_rows.at[pl.ds(0, total)]
    (pltpu.make_async_copy(local, remote, sem) if to_hbm else pltpu.make_async_copy(remote, local, sem)).wait()


def _dispatch_kernel(run_start_ref, n8_ref, off_ref, tail_start_ref, tail_len_ref, nused_ref,
                     h_ref, lpt_ref, x_hbm, buf, zbuf, sem, zsem):
    i = pl.program_id(0)
    nt = pl.num_programs(0)
    slot = i % 2
    meta = (run_start_ref, n8_ref, off_ref)

    @pl.when(i >= 2)
    def _():
        _wait_runs(i - 2, meta, buf.at[slot], x_hbm, sem.at[slot], True)

    hb = h_ref[...].astype(BF16)
    lp16 = lpt_ref[...].astype(jnp.int16)
    one = jnp.ones((PERM_CHUNK, ROW_TILE), BF16)
    for c in range(PERM_ROWS // PERM_CHUNK):
        r0 = c * PERM_CHUNK
        rows = (lax.broadcasted_iota(I32, (PERM_CHUNK, ROW_TILE), 0) + r0).astype(jnp.int16)
        perm = jnp.zeros((PERM_CHUNK, ROW_TILE), BF16)
        for k in range(TOP_K):
            perm = jnp.where(rows == lp16[k:k + 1, :], one, perm)
        buf[slot, r0:r0 + PERM_CHUNK, :] = _pack_rows(jnp.dot(perm, hb, preferred_element_type=F32),
                                                      already_bf16=True)

    _start_runs(i, meta, buf.at[slot], x_hbm, sem.at[slot], True)

    @pl.when(i == nt - 1)
    def _():
        zbuf[...] = jnp.zeros_like(zbuf)

        def tail_copy(e):
            n = pl.multiple_of(tail_len_ref[e], RUN_ALIGN)
            dst = x_hbm.at[pl.ds(pl.multiple_of(tail_start_ref[e], RUN_ALIGN), n)]
            return pltpu.make_async_copy(zbuf.at[pl.ds(0, n)], dst, zsem), n

        def start_tail(e, carry):
            copy, n = tail_copy(e)

            @pl.when(n > 0)
            def _():
                copy.start()
            return carry

        def wait_tail(e, carry):
            copy, n = tail_copy(e)

            @pl.when(n > 0)
            def _():
                copy.wait()
            return carry

        lax.fori_loop(0, N_EXPERTS, start_tail, 0)
        lax.fori_loop(0, N_EXPERTS, wait_tail, 0)

        def spare_copy(blk_i):
            dst = x_hbm.at[pl.ds(pl.multiple_of(blk_i * MOE_BLOCK, MOE_BLOCK), MOE_BLOCK)]
            return pltpu.make_async_copy(zbuf, dst, zsem)

        def start_spare(blk_i, carry):
            spare_copy(blk_i).start()
            return carry

        def wait_spare(blk_i, carry):
            spare_copy(blk_i).wait()
            return carry

        n_blocks = x_hbm.shape[0] // MOE_BLOCK
        lax.fori_loop(nused_ref[0], n_blocks, start_spare, 0)
        lax.fori_loop(nused_ref[0], n_blocks, wait_spare, 0)

        @pl.when(nt >= 2)
        def _():
            _wait_runs(i - 1, meta, buf.at[1 - slot], x_hbm, sem.at[1 - slot], True)
        _wait_runs(i, meta, buf.at[slot], x_hbm, sem.at[slot], True)


def _dispatch(meta, tails, n_used, h2, lp_t, n_rows):
    t, d = h2.shape
    nt = t // ROW_TILE
    grid_spec = pltpu.PrefetchScalarGridSpec(
        num_scalar_prefetch=6,
        grid=(nt,),
        in_specs=[
            pl.BlockSpec((ROW_TILE, d), lambda i, *_: (i, 0)),
            pl.BlockSpec((None, TOP_K, ROW_TILE), lambda i, *_: (i, 0, 0)),
        ],
        out_specs=pl.BlockSpec(memory_space=pl.ANY),
        scratch_shapes=[
            pltpu.VMEM((2, PERM_ROWS, HALF_W), WORD),
            pltpu.VMEM((MOE_BLOCK, HALF_W), WORD),
            pltpu.SemaphoreType.DMA((2,)),
            pltpu.SemaphoreType.DMA(()),
        ],
    )
    return pl.pallas_call(
        _dispatch_kernel,
        grid_spec=grid_spec,
        out_shape=jax.ShapeDtypeStruct((n_rows, HALF_W), WORD),
        compiler_params=pltpu.CompilerParams(
            dimension_semantics=("arbitrary",), vmem_limit_bytes=VMEM_LIMIT, has_side_effects=True),
        name="dispatch",
    )(*meta, *tails, n_used, h2, lp_t)


def _ffn_kernel(blk_e_ref, nused_ref, x_ref, wgu_ref, bgu_ref, wdn_ref, bdn_ref, y_ref, wgu_bf, wdn_bf):
    i = pl.program_id(0)
    used = i < nused_ref[0]
    new_expert = jnp.logical_or(i == 0, blk_e_ref[i] != blk_e_ref[jnp.maximum(i - 1, 0)])

    @pl.when(jnp.logical_and(used, new_expert))
    def _():
        wgu_bf[...] = wgu_ref[...].astype(BF16)
        wdn_bf[...] = wdn_ref[...].astype(BF16)

    @pl.when(used)
    def _():
        xb = _unpack_rows(x_ref[...])
        acc = jnp.zeros((MOE_BLOCK, D_MODEL), F32)
        for c in range(D_EXPERT // FFN_CHUNK):
            lo = c * FFN_CHUNK
            gate = jnp.dot(xb, wgu_bf[:, lo:lo + FFN_CHUNK], preferred_element_type=F32)
            gate = gate + bgu_ref[:, lo:lo + FFN_CHUNK]
            up = jnp.dot(xb, wgu_bf[:, D_EXPERT + lo:D_EXPERT + lo + FFN_CHUNK], preferred_element_type=F32)
            up = up + bgu_ref[:, D_EXPERT + lo:D_EXPERT + lo + FFN_CHUNK]
            gate = jnp.minimum(gate, SWIGLU_LIMIT)
            up = jnp.clip(up, -SWIGLU_LIMIT, SWIGLU_LIMIT)
            hmid = (up + 1.0) * gate * jax.nn.sigmoid(SWIGLU_ALPHA * gate)
            acc = acc + jnp.dot(hmid.astype(BF16), wdn_bf[lo:lo + FFN_CHUNK, :], preferred_element_type=F32)
        y_ref[...] = _pack_rows(acc + bdn_ref[...])

    @pl.when(i >= nused_ref[0])
    def _():
        y_ref[...] = jnp.zeros_like(y_ref)


def _expert_ffn(blk_e, n_used, x_rows, wgu, bgu, wdn, bdn):
    nblk = x_rows.shape[0] // MOE_BLOCK
    d = D_MODEL
    grid_spec = pltpu.PrefetchScalarGridSpec(
        num_scalar_prefetch=2,
        grid=(nblk,),
        in_specs=[
            pl.BlockSpec((MOE_BLOCK, HALF_W), lambda i, be, nu: (jnp.minimum(i, nu[0] - 1), 0)),
            pl.BlockSpec((None, d, 2 * D_EXPERT), lambda i, be, nu: (be[i], 0, 0)),
            pl.BlockSpec((None, 1, 2 * D_EXPERT), lambda i, be, nu: (be[i], 0, 0)),
            pl.BlockSpec((None, D_EXPERT, d), lambda i, be, nu: (be[i], 0, 0)),
            pl.BlockSpec((None, 1, d), lambda i, be, nu: (be[i], 0, 0)),
        ],
        out_specs=pl.BlockSpec((MOE_BLOCK, HALF_W), lambda i, be, nu: (i, 0)),
        scratch_shapes=[pltpu.VMEM((d, 2 * D_EXPERT), BF16), pltpu.VMEM((D_EXPERT, d), BF16)],
    )
    return pl.pallas_call(
        _ffn_kernel,
        grid_spec=grid_spec,
        out_shape=jax.ShapeDtypeStruct((nblk * MOE_BLOCK, HALF_W), WORD),
        compiler_params=pltpu.CompilerParams(
            dimension_semantics=("arbitrary",), vmem_limit_bytes=FFN_VMEM_LIMIT),
        name="expert_ffn",
    )(blk_e, n_used, x_rows, wgu, bgu, wdn, bdn)


def _combine_kernel(run_start_ref, n8_ref, off_ref, y_hbm, h_ref, lpt_ref, gwt_ref, g_ref, b_ref,
                    out_ref, ybuf, wperm_scr, ysel_scr, sem):
    i = pl.program_id(0)
    nt = pl.num_programs(0)
    slot = i % 2
    meta = (run_start_ref, n8_ref, off_ref)

    @pl.when(i == 0)
    def _():
        ybuf[...] = jnp.zeros_like(ybuf)
        _start_runs(0, meta, ybuf.at[0], y_hbm, sem.at[0], False)

    @pl.when(i + 1 < nt)
    def _():
        _start_runs(i + 1, meta, ybuf.at[1 - slot], y_hbm, sem.at[1 - slot], False)

    _wait_runs(i, meta, ybuf.at[slot], y_hbm, sem.at[slot], False)

    lp16 = lpt_ref[...].astype(jnp.int16)
    gw16 = gwt_ref[...].astype(BF16)
    for c in range(PERM_ROWS // PERM_CHUNK):
        r0 = c * PERM_CHUNK
        rows = (lax.broadcasted_iota(I32, (PERM_CHUNK, ROW_TILE), 0) + r0).astype(jnp.int16)
        wperm = jnp.zeros((PERM_CHUNK, ROW_TILE), BF16)
        for k in range(TOP_K):
            wperm = jnp.where(rows == lp16[k:k + 1, :], gw16[k:k + 1, :], wperm)
        wperm_scr[r0:r0 + PERM_CHUNK, :] = wperm
        ysel_scr[r0:r0 + PERM_CHUNK, :] = _unpack_rows(ybuf[slot, r0:r0 + PERM_CHUNK, :])
    half = ROW_TILE // 2
    for r in range(2):
        tok = slice(r * half, (r + 1) * half)
        ffn = lax.dot_general(wperm_scr[:, tok], ysel_scr[...], (((0,), (0,)), ((), ())),
                              preferred_element_type=F32)
        out_ref[tok, :] = _layer_norm(DEEPNORM_ALPHA * h_ref[tok, :] + ffn, g_ref[...], b_ref[...])


def _combine(meta, y_rows, h2, lp_t, gw_t, ln_g, ln_b):
    t, d = h2.shape
    nt = t // ROW_TILE
    grid_spec = pltpu.PrefetchScalarGridSpec(
        num_scalar_prefetch=3,
        grid=(nt,),
        in_specs=[
            pl.BlockSpec(memory_space=pl.ANY),
            pl.BlockSpec((ROW_TILE, d), lambda i, *_: (i, 0)),
            pl.BlockSpec((None, TOP_K, ROW_TILE), lambda i, *_: (i, 0, 0)),
            pl.BlockSpec((None, TOP_K, ROW_TILE), lambda i, *_: (i, 0, 0)),
            pl.BlockSpec((1, d), lambda i, *_: (0, 0)),
            pl.BlockSpec((1, d), lambda i, *_: (0, 0)),
        ],
        out_specs=pl.BlockSpec((ROW_TILE, d), lambda i, *_: (i, 0)),
        scratch_shapes=[pltpu.VMEM((2, PERM_ROWS, HALF_W), WORD), pltpu.VMEM((PERM_ROWS, ROW_TILE), BF16),
                        pltpu.VMEM((PERM_ROWS, d), BF16), pltpu.SemaphoreType.DMA((2,))],
    )
    return pl.pallas_call(
        _combine_kernel,
        grid_spec=grid_spec,
        out_shape=jax.ShapeDtypeStruct((t, d), F32),
        compiler_params=pltpu.CompilerParams(
            dimension_semantics=("arbitrary",), vmem_limit_bytes=VMEM_LIMIT),
        name="combine_ln",
    )(*meta, y_rows, h2, lp_t, gw_t, ln_g, ln_b)


def _max_blocks(t):
    nt = t // ROW_TILE
    rows = t * TOP_K + nt * N_EXPERTS * (RUN_ALIGN - 1) + N_EXPERTS * (MOE_BLOCK - RUN_ALIGN)
    return -(-rows // MOE_BLOCK)


def _layout(counts, nblk):
    n8 = (counts + RUN_ALIGN - 1) // RUN_ALIGN * RUN_ALIGN
    total = jnp.sum(n8, axis=0)
    padded = (total + MOE_BLOCK - 1) // MOE_BLOCK * MOE_BLOCK
    e_end = jnp.cumsum(padded)
    e_start = e_end - padded
    run_start = e_start[None, :] + jnp.cumsum(n8, axis=0) - n8
    off = jnp.cumsum(n8, axis=1) - n8
    blk_row0 = jnp.arange(nblk, dtype=I32) * MOE_BLOCK
    blk_e = jnp.minimum(jnp.sum((e_end[None, :] <= blk_row0[:, None]).astype(I32), axis=1), N_EXPERTS - 1)
    n_used = (e_end[-1] // MOE_BLOCK).reshape(1)
    meta = tuple(a.reshape(-1).astype(I32) for a in (run_start, n8, off))
    tails = ((e_start + total).astype(I32), (padded - total).astype(I32))
    return meta, tails, blk_e.astype(I32), n_used.astype(I32)


def kernel(x, w_in, attn_sinks, w_branch_a, w_branch_b, w_out, ln1_g, ln1_b, w_router, b_router,
           w_gate_up, b_gate_up, w_down, b_down, ln2_g, ln2_b):
    depth = w_in.shape[0]
    b, s, d = x.shape
    t = b * s
    nt = t // ROW_TILE

    pos = jnp.arange(s, dtype=F32)
    inv_freq = ROPE_THETA ** (-jnp.arange(0, HEAD_DIM, 2, dtype=F32) / HEAD_DIM)
    ang = pos[:, None] * inv_freq[None, :]
    cos32, sin32 = jnp.cos(ang), jnp.sin(ang)
    cos_t = jnp.tile(cos32, (1, LANES // (HEAD_DIM // 2)))
    sin_t = jnp.tile(jnp.concatenate([-sin32, sin32], axis=1), (1, LANES // HEAD_DIM))

    h = x
    for l in range(depth):
        wi = w_in[l]
        qscale = HEAD_DIM ** -0.5 * LOG2E
        cols = [wi[:, :A_Q_W] * qscale, wi[:, OFF_AK:OFF_BQ]]
        for g in range(len(B_GROUPS)):
            cols += [wi[:, OFF_BQ + g * B_GROUP_W:OFF_BQ + (g + 1) * B_GROUP_W] * qscale,
                     wi[:, OFF_BK + g * B_GROUP_W:OFF_BK + (g + 1) * B_GROUP_W],
                     wi[:, OFF_BV + g * B_GROUP_W:OFF_BV + (g + 1) * B_GROUP_W]]
        w_qkv = jnp.concatenate(cols, axis=1).astype(BF16)

        qkv_a, qkv_b0, qkv_b1, qkv_b2 = _qkv_rope(h, w_qkv, cos_t, sin_t)
        oa, ob = _attention(attn_sinks[l].astype(F32) * LOG2E, qkv_a, qkv_b0, qkv_b1, qkv_b2)

        wr = jnp.zeros((d, ROUTER_PAD), F32).at[:, :N_EXPERTS].set(w_router[l]).astype(BF16)
        br = jnp.full((1, ROUTER_PAD), NEG, F32).at[0, :N_EXPERTS].set(b_router[l].astype(F32))
        h1, lp_t, gw_t, counts = _merge(
            h.reshape(t, d), oa.reshape(t, A_Q_W), ob.reshape(t, B_GROUP_W),
            wi[:, OFF_G:].astype(BF16), w_branch_a[l].astype(BF16), w_branch_b[l].astype(BF16),
            w_out[l].astype(BF16), wr, br, ln1_g[l].reshape(1, d), ln1_b[l].reshape(1, d))

        nblk = _max_blocks(t)
        meta, tails, blk_e, n_used = _layout(counts.reshape(nt, ROUTER_PAD)[:, :N_EXPERTS].astype(I32), nblk)
        x_rows = _dispatch(meta, tails, n_used, h1, lp_t, nblk * MOE_BLOCK)
        y_rows = _expert_ffn(
            blk_e, n_used, x_rows,
            w_gate_up[l], b_gate_up[l].reshape(N_EXPERTS, 1, 2 * D_EXPERT),
            w_down[l], b_down[l].reshape(N_EXPERTS, 1, d))
        h = _combine(meta, y_rows, h1, lp_t, gw_t, ln2_g[l].reshape(1, d), ln2_b[l].reshape(1, d)).reshape(b, s, d)
    return h
```

```python
import jax
import jax.numpy as jnp
from jax import lax
from jax.experimental import pallas as pl
from jax.experimental.pallas import tpu as pltpu

F32 = jnp.float32
BF16 = jnp.bfloat16
I32 = jnp.int32
WORD = jnp.uint32

D_MODEL = 1024
HEAD_DIM = 64
ROPE_THETA = 10000.0
A_Q_W = 1024
A_KV_W = 256
A_KV_HEADS = 4
A_GQA = 4
A_WINDOW = 128
B_GROUPS = ((128, 1), (512, 4), (2048, 16))
B_GROUP_W = 256
B_HEADS = 4
B_W = 768
OFF_AK = A_Q_W
OFF_AV = OFF_AK + A_KV_W
OFF_BQ = OFF_AV + A_KV_W
OFF_BK = OFF_BQ + B_W
OFF_BV = OFF_BK + B_W
OFF_G = OFF_BV + B_W
N_EXPERTS = 32
TOP_K = 4
D_EXPERT = 1024
SWIGLU_LIMIT = 7.0
SWIGLU_ALPHA = 1.702
LN_EPS = 1e-5
DEEPNORM_ALPHA = 2.0 ** 0.25
LOG2E = 1.4426950408889634

LANES = 128
SUBLANES = 8
ATTN_BLOCK = 128
SEQ_TILE = 512
ROW_TILE = 512
MOE_BLOCK = 1024
FFN_CHUNK = 512
RUN_ALIGN = SUBLANES
PERM_ROWS = ROW_TILE * TOP_K + N_EXPERTS * RUN_ALIGN
PERM_CHUNK = 256
A_SEG_W = A_Q_W + 2 * A_KV_W
B_SEG_W = 3 * B_GROUP_W
QKV_W = A_SEG_W + 3 * B_SEG_W
HALF_W = D_MODEL // 2
ROUTER_PAD = LANES
NEG = -1e30
VMEM_LIMIT = 48 * 1024 * 1024
FFN_VMEM_LIMIT = 56 * 1024 * 1024


def _qkv_rope_kernel(x_ref, w_ref, cos_ref, sin_ref, a_ref, b0_ref, b1_ref, b2_ref, scr_ref):
    tm = x_ref.shape[0]
    xb = x_ref[...].astype(BF16)
    cos = cos_ref[...]
    sin = sin_ref[...]
    lane = lax.broadcasted_iota(I32, (tm, LANES), 1)
    first_half = (lane % HEAD_DIM) < (HEAD_DIM // 2)

    def rope(a):
        partner = jnp.where(first_half, pltpu.roll(a, LANES - 32, 1), pltpu.roll(a, 32, 1))
        return a * cos + partner * sin

    def segment(col0, width, n_rope_chunks):
        acc = jnp.dot(xb, w_ref[:, col0:col0 + width], preferred_element_type=F32)
        pieces = []
        for c in range(width // LANES):
            a = acc[:, c * LANES:(c + 1) * LANES]
            pieces.append(rope(a) if c < n_rope_chunks else a)
        return pieces

    a_ref[...] = jnp.concatenate(segment(0, A_SEG_W, (A_Q_W + A_KV_W) // LANES), axis=1).astype(BF16)
    b0_ref[...] = jnp.concatenate(segment(A_SEG_W, B_SEG_W, 2 * B_GROUP_W // LANES), axis=1).astype(BF16)
    n_chunks = B_SEG_W // LANES
    for out_ref, col0, dil in ((b1_ref, A_SEG_W + B_SEG_W, 4), (b2_ref, A_SEG_W + 2 * B_SEG_W, 16)):
        for k, piece in enumerate(segment(col0, B_SEG_W, 2 * B_GROUP_W // LANES)):
            scr_ref[k] = piece
        for c in range(dil):
            rows = [scr_ref[k, pl.ds(c, tm // dil, stride=dil), :] for k in range(n_chunks)]
            out_ref[c] = jnp.concatenate(rows, axis=1).astype(BF16)


def _qkv_rope(x, w_qkv, cos_t, sin_t):
    b, s, d = x.shape
    tm = SEQ_TILE
    grid = (b, s // tm)
    return pl.pallas_call(
        _qkv_rope_kernel,
        grid=grid,
        in_specs=[
            pl.BlockSpec((None, tm, d), lambda bi, si: (bi, si, 0)),
            pl.BlockSpec((d, QKV_W), lambda bi, si: (0, 0)),
            pl.BlockSpec((tm, LANES), lambda bi, si: (si, 0)),
            pl.BlockSpec((tm, LANES), lambda bi, si: (si, 0)),
        ],
        out_specs=[
            pl.BlockSpec((None, tm, A_SEG_W), lambda bi, si: (bi, si, 0)),
            pl.BlockSpec((None, tm, B_SEG_W), lambda bi, si: (bi, si, 0)),
            pl.BlockSpec((None, 4, tm // 4, B_SEG_W), lambda bi, si: (bi, 0, si, 0)),
            pl.BlockSpec((None, 16, tm // 16, B_SEG_W), lambda bi, si: (bi, 0, si, 0)),
        ],
        out_shape=[
            jax.ShapeDtypeStruct((b, s, A_SEG_W), BF16),
            jax.ShapeDtypeStruct((b, s, B_SEG_W), BF16),
            jax.ShapeDtypeStruct((b, 4, s // 4, B_SEG_W), BF16),
            jax.ShapeDtypeStruct((b, 16, s // 16, B_SEG_W), BF16),
        ],
        scratch_shapes=[pltpu.VMEM((B_SEG_W // LANES, tm, LANES), F32)],
        compiler_params=pltpu.CompilerParams(
            dimension_semantics=("arbitrary", "arbitrary"), vmem_limit_bytes=VMEM_LIMIT),
        name="qkv_rope",
    )(x, w_qkv, cos_t, sin_t)


def _stacked_heads_unit(q, kmat, vmat, bias, sink_slot=False):
    blk = q.shape[0]
    group = lax.broadcasted_iota(I32, (1, B_GROUP_W), 1) // HEAD_DIM
    zero = jnp.zeros_like(q)
    qs = jnp.concatenate([jnp.where(group == h, q, zero) for h in range(B_HEADS)], axis=0)
    if sink_slot:
        tile = 2 * SUBLANES
        not_slot = lax.broadcasted_iota(I32, (tile, B_GROUP_W), 0) > 0
        kmat = jnp.concatenate([jnp.where(not_slot, kmat[:tile], jnp.zeros_like(kmat[:tile])), kmat[tile:]], axis=0)
        vmat = jnp.concatenate([jnp.where(not_slot, vmat[:tile], jnp.zeros_like(vmat[:tile])), vmat[tile:]], axis=0)
    s = lax.dot_general(qs, kmat, (((1,), (1,)), ((), ())), preferred_element_type=F32) + bias
    m = jnp.max(s, axis=-1, keepdims=True)
    p = jnp.exp2(s - m)
    l = jnp.sum(p, axis=-1, keepdims=True)
    o = jnp.dot(p.astype(BF16), vmat, preferred_element_type=F32)
    out = jnp.zeros((blk, B_GROUP_W), F32)
    l_b = jnp.ones((blk, B_GROUP_W), F32)
    m_b = jnp.zeros((blk, B_GROUP_W), F32)
    for h in range(B_HEADS):
        out = jnp.where(group == h, o[h * blk:(h + 1) * blk], out)
        l_b = jnp.where(group == h, l[h * blk:(h + 1) * blk], l_b)
        m_b = jnp.where(group == h, m[h * blk:(h + 1) * blk], m_b)
    return out * (1.0 / l_b), m_b + jnp.log2(l_b)


def _store_lane_chunks(ref, lead, rows, x):
    for c in range(x.shape[1] // LANES):
        ref[(*lead, c, rows, slice(None))] = x[:, c * LANES:(c + 1) * LANES]


def _attention_kernel(sink_ref, a_cur, a_prev, b0_cur, b0_prev, b1_cur, b1_prev, b2_ref,
                      oa_ref, ob_ref, kexp_scr, vexp_scr, kvb_scr, o_scr, lse_scr, o2_scr, lse2_scr,
                      band_scr, sinkband_scr, causal_scr):
    j = pl.program_id(1)
    blk = ATTN_BLOCK
    nqb = SEQ_TILE // blk
    stack = B_HEADS * blk

    @pl.when(jnp.logical_and(pl.program_id(0) == 0, j == 0))
    def _():
        row = lax.broadcasted_iota(I32, (stack, 2 * blk), 0) % blk
        col = lax.broadcasted_iota(I32, (stack, 2 * blk), 1)
        row_block = lax.broadcasted_iota(I32, (stack, 2 * blk), 0) // blk
        dist = blk + row - col
        in_band = (dist >= 0) & (dist <= B_GROUPS[0][0] // B_GROUPS[0][1])
        band_scr[0] = jnp.where(in_band, 0.0, NEG)
        band_scr[1] = jnp.where(in_band & (col >= blk), 0.0, NEG)
        in_band = (dist >= 0) & (dist <= A_WINDOW - 1)
        for kvh in range(A_KV_HEADS):
            sink = jnp.zeros((stack, 2 * blk), F32)
            for g in range(A_GQA):
                sink = jnp.where(row_block == g, sink_ref[kvh * A_GQA + g], sink)
            sinkband_scr[2 * kvh] = jnp.where(col == 0, sink, jnp.where(in_band, 0.0, NEG))
            sinkband_scr[2 * kvh + 1] = jnp.where(col == 0, sink, jnp.where(in_band & (col >= blk), 0.0, NEG))
        r1 = lax.broadcasted_iota(I32, (stack, blk), 0) % blk
        c1 = lax.broadcasted_iota(I32, (stack, blk), 1)
        causal_scr[...] = jnp.where(c1 <= r1, 0.0, NEG)

    @pl.when(j == 0)
    def _():
        def class_body(c, carry):
            t = b2_ref[c]
            out, lse_b = _stacked_heads_unit(t[:, :B_GROUP_W], t[:, B_GROUP_W:2 * B_GROUP_W],
                                             t[:, 2 * B_GROUP_W:], causal_scr[...])
            rows = pl.ds(c, blk, stride=16)
            _store_lane_chunks(o2_scr, (), rows, out)
            _store_lane_chunks(lse2_scr, (), rows, lse_b)
            return carry

        lax.fori_loop(0, 16, class_body, 0, unroll=8)

    for src, r0, r1 in ((a_prev, 0, blk), (a_cur, blk, blk + SEQ_TILE)):
        for dst, col0 in ((kexp_scr, A_Q_W), (vexp_scr, A_Q_W + A_KV_W)):
            heads = src[:, col0:col0 + A_KV_W]
            for kvh in range(A_KV_HEADS):
                head = heads[:, kvh * HEAD_DIM:(kvh + 1) * HEAD_DIM]
                dst[r0:r1, kvh * B_GROUP_W:(kvh + 1) * B_GROUP_W] = jnp.concatenate([head] * A_GQA, axis=1)
    kvb_scr[0:blk, :] = b0_prev[:, B_GROUP_W:]
    kvb_scr[blk:, :] = b0_cur[:, B_GROUP_W:]

    def qblock_body(qb, carry):
        r0 = pl.multiple_of(qb * blk, blk)
        first = jnp.logical_and(j == 0, qb == 0).astype(I32)
        for kvh in range(A_KV_HEADS):
            lanes = slice(kvh * B_GROUP_W, (kvh + 1) * B_GROUP_W)
            out, _ = _stacked_heads_unit(a_cur[pl.ds(r0, blk), lanes], kexp_scr[pl.ds(r0, 2 * blk), lanes],
                                         vexp_scr[pl.ds(r0, 2 * blk), lanes], sinkband_scr[2 * kvh + first],
                                         sink_slot=True)
            oa_ref[pl.ds(r0, blk), lanes] = out.astype(oa_ref.dtype)
        out, lse_b = _stacked_heads_unit(b0_cur[pl.ds(r0, blk), 0:B_GROUP_W],
                                         kvb_scr[pl.ds(r0, 2 * blk), 0:B_GROUP_W],
                                         kvb_scr[pl.ds(r0, 2 * blk), B_GROUP_W:], band_scr[first])
        _store_lane_chunks(o_scr, (0,), pl.ds(r0, blk), out)
        _store_lane_chunks(lse_scr, (0,), pl.ds(r0, blk), lse_b)
        return carry

    lax.fori_loop(0, nqb, qblock_body, 0, unroll=True)

    first_class_block = (j == 0).astype(I32)

    def class4_body(c, carry):
        cur = b1_cur[c]
        prev = b1_prev[c]
        kv = jnp.concatenate([prev[:, B_GROUP_W:], cur[:, B_GROUP_W:]], axis=0)
        out, lse_b = _stacked_heads_unit(cur[:, :B_GROUP_W], kv[:, :B_GROUP_W], kv[:, B_GROUP_W:],
                                         band_scr[first_class_block])
        rows = pl.ds(c, blk, stride=4)
        _store_lane_chunks(o_scr, (1,), rows, out)
        _store_lane_chunks(lse_scr, (1,), rows, lse_b)
        return carry

    lax.fori_loop(0, 4, class4_body, 0, unroll=True)

    t0 = pl.multiple_of(j * SEQ_TILE, SEQ_TILE)
    for c in range(B_GROUP_W // LANES):
        l0, l1, l2 = lse_scr[0, c], lse_scr[1, c], lse2_scr[c, pl.ds(t0, SEQ_TILE), :]
        mx = jnp.maximum(jnp.maximum(l0, l1), l2)
        w0, w1, w2 = jnp.exp2(l0 - mx), jnp.exp2(l1 - mx), jnp.exp2(l2 - mx)
        num = w0 * o_scr[0, c] + w1 * o_scr[1, c] + w2 * o2_scr[c, pl.ds(t0, SEQ_TILE), :]
        ob_ref[:, c * LANES:(c + 1) * LANES] = (num / (w0 + w1 + w2)).astype(ob_ref.dtype)


def _attention(sinks, qkv_a, qkv_b0, qkv_b1, qkv_b2):
    b, s, _ = qkv_a.shape
    t = SEQ_TILE
    blk = ATTN_BLOCK
    nq = t // blk
    grid = (b, s // t)
    n_chunks = B_GROUP_W // LANES
    return pl.pallas_call(
        _attention_kernel,
        grid=grid,
        in_specs=[
            pl.BlockSpec(memory_space=pltpu.SMEM),
            pl.BlockSpec((None, t, A_SEG_W), lambda bi, j: (bi, j, 0)),
            pl.BlockSpec((None, blk, A_SEG_W), lambda bi, j: (bi, jnp.maximum(j * nq - 1, 0), 0)),
            pl.BlockSpec((None, t, B_SEG_W), lambda bi, j: (bi, j, 0)),
            pl.BlockSpec((None, blk, B_SEG_W), lambda bi, j: (bi, jnp.maximum(j * nq - 1, 0), 0)),
            pl.BlockSpec((None, 4, blk, B_SEG_W), lambda bi, j: (bi, 0, j, 0)),
            pl.BlockSpec((None, 4, blk, B_SEG_W), lambda bi, j: (bi, 0, jnp.maximum(j - 1, 0), 0)),
            pl.BlockSpec((None, 16, blk, B_SEG_W), lambda bi, j: (bi, 0, 0, 0)),
        ],
        out_specs=[
            pl.BlockSpec((None, t, A_Q_W), lambda bi, j: (bi, j, 0)),
            pl.BlockSpec((None, t, B_GROUP_W), lambda bi, j: (bi, j, 0)),
        ],
        out_shape=[
            jax.ShapeDtypeStruct((b, s, A_Q_W), BF16),
            jax.ShapeDtypeStruct((b, s, B_GROUP_W), BF16),
        ],
        scratch_shapes=[
            pltpu.VMEM((blk + t, A_KV_HEADS * B_GROUP_W), BF16),
            pltpu.VMEM((blk + t, A_KV_HEADS * B_GROUP_W), BF16),
            pltpu.VMEM((blk + t, 2 * B_GROUP_W), BF16),
            pltpu.VMEM((2, n_chunks, t, LANES), F32),
            pltpu.VMEM((2, n_chunks, t, LANES), F32),
            pltpu.VMEM((n_chunks, s, LANES), F32),
            pltpu.VMEM((n_chunks, s, LANES), F32),
            pltpu.VMEM((2, B_HEADS * blk, 2 * blk), F32),
            pltpu.VMEM((2 * A_KV_HEADS, B_HEADS * blk, 2 * blk), F32),
            pltpu.VMEM((B_HEADS * blk, blk), F32),
        ],
        compiler_params=pltpu.CompilerParams(
            dimension_semantics=("arbitrary", "arbitrary"), vmem_limit_bytes=VMEM_LIMIT),
        name="banded_attention",
    )(sinks, qkv_a, qkv_a, qkv_b0, qkv_b0, qkv_b1, qkv_b1, qkv_b2)


def _layer_norm(z, g, b):
    mu = jnp.mean(z, axis=-1, keepdims=True)
    zc = z - mu
    var = jnp.mean(zc * zc, axis=-1, keepdims=True)
    return zc * lax.rsqrt(var + LN_EPS) * g + b


def _merge_kernel(x_ref, oa_ref, ob_ref, wg_ref, wa_ref, wb_ref, wo_ref, wr_ref, br_ref, g_ref, b_ref,
                  h_ref, lp_ref, gw_ref, cnt_ref, logits_scr):
    i = pl.program_id(0)

    @pl.when(i == 0)
    def _():
        logits_scr[...] = jnp.zeros_like(logits_scr)

    logits = logits_scr[(i + 1) % 2]

    x = x_ref[...]
    xb = x.astype(BF16)
    gates = jax.nn.sigmoid(jnp.dot(xb, wg_ref[...], preferred_element_type=F32))
    ya = jnp.dot(oa_ref[...], wa_ref[...], preferred_element_type=F32)
    yb = jnp.dot(ob_ref[...], wb_ref[...], preferred_element_type=F32)
    merged = gates[:, :D_MODEL] * ya + gates[:, D_MODEL:] * yb
    mix = jnp.dot(merged.astype(BF16), wo_ref[...], preferred_element_type=F32)
    h = _layer_norm(DEEPNORM_ALPHA * x + mix, g_ref[...], b_ref[...])
    h_ref[...] = h
    logits_scr[i % 2] = jnp.dot(h.astype(BF16), wr_ref[...], preferred_element_type=F32) + br_ref[...]

    tm = logits.shape[0]
    lane = lax.broadcasted_iota(I32, (tm, ROUTER_PAD), 1)
    lane_f = lane.astype(F32)
    v_out = jnp.full((tm, ROUTER_PAD), NEG, F32)
    onehots = []
    lg = logits
    for k in range(TOP_K):
        m = jnp.max(lg, axis=-1, keepdims=True)
        idx = jnp.min(jnp.where(lg == m, lane_f, float(ROUTER_PAD)), axis=-1, keepdims=True)
        chosen = lane_f == idx
        onehots.append(chosen.astype(F32))
        v_out = jnp.where(lane == k, m, v_out)
        lg = jnp.where(chosen, 3.0 * NEG, lg)
    vmax = jnp.max(v_out, axis=-1, keepdims=True)
    ex = jnp.exp(v_out - vmax)
    gw_ref[...] = jnp.transpose(ex / jnp.sum(ex, axis=-1, keepdims=True))[:TOP_K]

    routed = onehots[0] + onehots[1] + onehots[2] + onehots[3]
    r_i = lax.broadcasted_iota(I32, (tm, tm), 0)
    c_i = lax.broadcasted_iota(I32, (tm, tm), 1)
    before = jnp.dot((c_i < r_i).astype(BF16), routed.astype(BF16), preferred_element_type=F32)
    cnt = jnp.sum(routed, axis=0, keepdims=True)
    runs = jnp.ceil(cnt * (1.0 / RUN_ALIGN))
    e_r = lax.broadcasted_iota(I32, (ROUTER_PAD, ROUTER_PAD), 0)
    e_c = lax.broadcasted_iota(I32, (ROUTER_PAD, ROUTER_PAD), 1)
    off = jnp.dot(jnp.broadcast_to(runs, (SUBLANES, ROUTER_PAD)).astype(BF16), (e_r < e_c).astype(BF16),
                  preferred_element_type=F32)[0:1] * float(RUN_ALIGN)
    base = before + off
    lp = jnp.zeros((tm, ROUTER_PAD), F32)
    for k in range(TOP_K):
        lp = jnp.where(lane == k, jnp.sum(onehots[k] * base, axis=-1, keepdims=True), lp)
    lp_ref[...] = jnp.transpose(lp)[:TOP_K].astype(I32)
    cnt_ref[...] = cnt


def _merge(x2, oa2, ob2, wg, wa, wb, wo, wr, br, ln_g, ln_b):
    t, d = x2.shape
    tm = ROW_TILE
    nt = t // tm
    full = lambda shape: pl.BlockSpec(shape, lambda i: (0, 0))
    dense = lambda i: (jnp.minimum(i, nt - 1), 0)
    routed = lambda i: (jnp.maximum(i - 1, 0), 0, 0)
    return pl.pallas_call(
        _merge_kernel,
        grid=(nt + 1,),
        in_specs=[
            pl.BlockSpec((tm, d), dense),
            pl.BlockSpec((tm, A_Q_W), dense),
            pl.BlockSpec((tm, B_GROUP_W), dense),
            full((d, 2 * d)), full((A_Q_W, d)), full((B_GROUP_W, d)), full((d, d)),
            full((d, ROUTER_PAD)), full((1, ROUTER_PAD)), full((1, d)), full((1, d)),
        ],
        out_specs=[
            pl.BlockSpec((tm, d), dense),
            pl.BlockSpec((None, TOP_K, tm), routed),
            pl.BlockSpec((None, TOP_K, tm), routed),
            pl.BlockSpec((None, 1, ROUTER_PAD), routed),
        ],
        out_shape=[
            jax.ShapeDtypeStruct((t, d), F32),
            jax.ShapeDtypeStruct((nt, TOP_K, tm), I32),
            jax.ShapeDtypeStruct((nt, TOP_K, tm), F32),
            jax.ShapeDtypeStruct((nt, 1, ROUTER_PAD), F32),
        ],
        scratch_shapes=[pltpu.VMEM((2, tm, ROUTER_PAD), F32)],
        compiler_params=pltpu.CompilerParams(
            dimension_semantics=("arbitrary",), vmem_limit_bytes=VMEM_LIMIT),
        name="merge_ln_router",
    )(x2, oa2, ob2, wg, wa, wb, wo, wr, br, ln_g, ln_b)


def _pack_rows(v, already_bf16=False):
    vb = v if already_bf16 else v.astype(BF16).astype(F32)
    lo = lax.bitcast_convert_type(vb[:, :HALF_W], WORD) >> 16
    hi = lax.bitcast_convert_type(vb[:, HALF_W:], WORD) & jnp.uint32(0xFFFF0000)
    return lo | hi


def _unpack_rows(w):
    lo = lax.bitcast_convert_type(w << 16, F32)
    hi = lax.bitcast_convert_type(w & jnp.uint32(0xFFFF0000), F32)
    return jnp.concatenate([lo, hi], axis=1).astype(BF16)


def _run_copy(tile, e, meta, vmem_rows, hbm_rows, sem, to_hbm):
    run_start_ref, n8_ref, off_ref = meta
    idx = tile * N_EXPERTS + e
    n = pl.multiple_of(n8_ref[idx], RUN_ALIGN)
    local = vmem_rows.at[pl.ds(pl.multiple_of(off_ref[idx], RUN_ALIGN), n)]
    remote = hbm_rows.at[pl.ds(pl.multiple_of(run_start_ref[idx], RUN_ALIGN), n)]
    return (pltpu.make_async_copy(local, remote, sem) if to_hbm
            else pltpu.make_async_copy(remote, local, sem)), n


def _start_runs(tile, meta, vmem_rows, hbm_rows, sem, to_hbm):
    def body(e, carry):
        copy, n = _run_copy(tile, e, meta, vmem_rows, hbm_rows, sem, to_hbm)

        @pl.when(n > 0)
        def _():
            copy.start()
        return carry
    lax.fori_loop(0, N_EXPERTS, body, 0, unroll=4)


def _wait_runs(tile, meta, vmem_rows, hbm_rows, sem, to_hbm):
    _, n8_ref, off_ref = meta
    last = tile * N_EXPERTS + N_EXPERTS - 1
    total = pl.multiple_of(off_ref[last] + n8_ref[last], RUN_ALIGN)
    local = vmem_rows.at[pl.ds(0, total)]
    remote = hbm_rows.at[pl.ds(0, total)]
    (pltpu.make_async_copy(local, remote, sem) if to_hbm else pltpu.make_async_copy(remote, local, sem)).wait()


def _dispatch_kernel(run_start_ref, n8_ref, off_ref, tail_start_ref, tail_len_ref, nused_ref,
                     h_ref, lpt_ref, x_hbm, buf, zbuf, sem, zsem):
    i = pl.program_id(0)
    nt = pl.num_programs(0)
    slot = i % 2
    meta = (run_start_ref, n8_ref, off_ref)

    @pl.when(i >= 2)
    def _():
        _wait_runs(i - 2, meta, buf.at[slot], x_hbm, sem.at[slot], True)

    hb = h_ref[...].astype(BF16)
    lp16 = lpt_ref[...].astype(jnp.int16)
    one = jnp.ones((PERM_CHUNK, ROW_TILE), BF16)
    for c in range(PERM_ROWS // PERM_CHUNK):
        r0 = c * PERM_CHUNK
        rows = (lax.broadcasted_iota(I32, (PERM_CHUNK, ROW_TILE), 0) + r0).astype(jnp.int16)
        perm = jnp.zeros((PERM_CHUNK, ROW_TILE), BF16)
        for k in range(TOP_K):
            perm = jnp.where(rows == lp16[k:k + 1, :], one, perm)
        buf[slot, r0:r0 + PERM_CHUNK, :] = _pack_rows(jnp.dot(perm, hb, preferred_element_type=F32),
                                                      already_bf16=True)

    _start_runs(i, meta, buf.at[slot], x_hbm, sem.at[slot], True)

    @pl.when(i == nt - 1)
    def _():
        zbuf[...] = jnp.zeros_like(zbuf)

        def tail_copy(e):
            n = pl.multiple_of(tail_len_ref[e], RUN_ALIGN)
            dst = x_hbm.at[pl.ds(pl.multiple_of(tail_start_ref[e], RUN_ALIGN), n)]
            return pltpu.make_async_copy(zbuf.at[pl.ds(0, n)], dst, zsem), n

        def start_tail(e, carry):
            copy, n = tail_copy(e)

            @pl.when(n > 0)
            def _():
                copy.start()
            return carry

        def wait_tail(e, carry):
            copy, n = tail_copy(e)

            @pl.when(n > 0)
            def _():
                copy.wait()
            return carry

        lax.fori_loop(0, N_EXPERTS, start_tail, 0)
        lax.fori_loop(0, N_EXPERTS, wait_tail, 0)

        def spare_copy(blk_i):
            dst = x_hbm.at[pl.ds(pl.multiple_of(blk_i * MOE_BLOCK, MOE_BLOCK), MOE_BLOCK)]
            return pltpu.make_async_copy(zbuf, dst, zsem)

        def start_spare(blk_i, carry):
            spare_copy(blk_i).start()
            return carry

        def wait_spare(blk_i, carry):
            spare_copy(blk_i).wait()
            return carry

        n_blocks = x_hbm.shape[0] // MOE_BLOCK
        lax.fori_loop(nused_ref[0], n_blocks, start_spare, 0)
        lax.fori_loop(nused_ref[0], n_blocks, wait_spare, 0)

        @pl.when(nt >= 2)
        def _():
            _wait_runs(i - 1, meta, buf.at[1 - slot], x_hbm, sem.at[1 - slot], True)
        _wait_runs(i, meta, buf.at[slot], x_hbm, sem.at[slot], True)


def _dispatch(meta, tails, n_used, h2, lp_t, n_rows):
    t, d = h2.shape
    nt = t // ROW_TILE
    grid_spec = pltpu.PrefetchScalarGridSpec(
        num_scalar_prefetch=6,
        grid=(nt,),
        in_specs=[
            pl.BlockSpec((ROW_TILE, d), lambda i, *_: (i, 0)),
            pl.BlockSpec((None, TOP_K, ROW_TILE), lambda i, *_: (i, 0, 0)),
        ],
        out_specs=pl.BlockSpec(memory_space=pl.ANY),
        scratch_shapes=[
            pltpu.VMEM((2, PERM_ROWS, HALF_W), WORD),
            pltpu.VMEM((MOE_BLOCK, HALF_W), WORD),
            pltpu.SemaphoreType.DMA((2,)),
            pltpu.SemaphoreType.DMA(()),
        ],
    )
    return pl.pallas_call(
        _dispatch_kernel,
        grid_spec=grid_spec,
        out_shape=jax.ShapeDtypeStruct((n_rows, HALF_W), WORD),
        compiler_params=pltpu.CompilerParams(
            dimension_semantics=("arbitrary",), vmem_limit_bytes=VMEM_LIMIT, has_side_effects=True),
        name="dispatch",
    )(*meta, *tails, n_used, h2, lp_t)


def _ffn_kernel(blk_e_ref, nused_ref, x_ref, wgu_ref, bgu_ref, wdn_ref, bdn_ref, y_ref, wgu_bf, wdn_bf):
    i = pl.program_id(0)
    used = i < nused_ref[0]
    new_expert = jnp.logical_or(i == 0, blk_e_ref[i] != blk_e_ref[jnp.maximum(i - 1, 0)])

    @pl.when(jnp.logical_and(used, new_expert))
    def _():
        wgu_bf[...] = wgu_ref[...].astype(BF16)
        wdn_bf[...] = wdn_ref[...].astype(BF16)

    @pl.when(used)
    def _():
        xb = _unpack_rows(x_ref[...])
        acc = jnp.zeros((MOE_BLOCK, D_MODEL), F32)
        for c in range(D_EXPERT // FFN_CHUNK):
            lo = c * FFN_CHUNK
            gate = jnp.dot(xb, wgu_bf[:, lo:lo + FFN_CHUNK], preferred_element_type=F32)
            gate = gate + bgu_ref[:, lo:lo + FFN_CHUNK]
            up = jnp.dot(xb, wgu_bf[:, D_EXPERT + lo:D_EXPERT + lo + FFN_CHUNK], preferred_element_type=F32)
            up = up + bgu_ref[:, D_EXPERT + lo:D_EXPERT + lo + FFN_CHUNK]
            gate = jnp.minimum(gate, SWIGLU_LIMIT)
            up = jnp.clip(up, -SWIGLU_LIMIT, SWIGLU_LIMIT)
            hmid = (up + 1.0) * gate * jax.nn.sigmoid(SWIGLU_ALPHA * gate)
            acc = acc + jnp.dot(hmid.astype(BF16), wdn_bf[lo:lo + FFN_CHUNK, :], preferred_element_type=F32)
        y_ref[...] = _pack_rows(acc + bdn_ref[...])

    @pl.when(i >= nused_ref[0])
    def _():
        y_ref[...] = jnp.zeros_like(y_ref)


def _expert_ffn(blk_e, n_used, x_rows, wgu, bgu, wdn, bdn):
    nblk = x_rows.shape[0] // MOE_BLOCK
    d = D_MODEL
    grid_spec = pltpu.PrefetchScalarGridSpec(
        num_scalar_prefetch=2,
        grid=(nblk,),
        in_specs=[
            pl.BlockSpec((MOE_BLOCK, HALF_W), lambda i, be, nu: (jnp.minimum(i, nu[0] - 1), 0)),
            pl.BlockSpec((None, d, 2 * D_EXPERT), lambda i, be, nu: (be[i], 0, 0)),
            pl.BlockSpec((None, 1, 2 * D_EXPERT), lambda i, be, nu: (be[i], 0, 0)),
            pl.BlockSpec((None, D_EXPERT, d), lambda i, be, nu: (be[i], 0, 0)),
            pl.BlockSpec((None, 1, d), lambda i, be, nu: (be[i], 0, 0)),
        ],
        out_specs=pl.BlockSpec((MOE_BLOCK, HALF_W), lambda i, be, nu: (i, 0)),
        scratch_shapes=[pltpu.VMEM((d, 2 * D_EXPERT), BF16), pltpu.VMEM((D_EXPERT, d), BF16)],
    )
    return pl.pallas_call(
        _ffn_kernel,
        grid_spec=grid_spec,
        out_shape=jax.ShapeDtypeStruct((nblk * MOE_BLOCK, HALF_W), WORD),
        compiler_params=pltpu.CompilerParams(
            dimension_semantics=("arbitrary",), vmem_limit_bytes=FFN_VMEM_LIMIT),
        name="expert_ffn",
    )(blk_e, n_used, x_rows, wgu, bgu, wdn, bdn)


def _combine_kernel(run_start_ref, n8_ref, off_ref, y_hbm, h_ref, lpt_ref, gwt_ref, g_ref, b_ref,
                    out_ref, ybuf, wperm_scr, ysel_scr, sem):
    i = pl.program_id(0)
    nt = pl.num_programs(0)
    slot = i % 2
    meta = (run_start_ref, n8_ref, off_ref)

    @pl.when(i == 0)
    def _():
        ybuf[...] = jnp.zeros_like(ybuf)
        _start_runs(0, meta, ybuf.at[0], y_hbm, sem.at[0], False)

    @pl.when(i + 1 < nt)
    def _():
        _start_runs(i + 1, meta, ybuf.at[1 - slot], y_hbm, sem.at[1 - slot], False)

    _wait_runs(i, meta, ybuf.at[slot], y_hbm, sem.at[slot], False)

    lp16 = lpt_ref[...].astype(jnp.int16)
    gw16 = gwt_ref[...].astype(BF16)
    for c in range(PERM_ROWS // PERM_CHUNK):
        r0 = c * PERM_CHUNK
        rows = (lax.broadcasted_iota(I32, (PERM_CHUNK, ROW_TILE), 0) + r0).astype(jnp.int16)
        wperm = jnp.zeros((PERM_CHUNK, ROW_TILE), BF16)
        for k in range(TOP_K):
            wperm = jnp.where(rows == lp16[k:k + 1, :], gw16[k:k + 1, :], wperm)
        wperm_scr[r0:r0 + PERM_CHUNK, :] = wperm
        ysel_scr[r0:r0 + PERM_CHUNK, :] = _unpack_rows(ybuf[slot, r0:r0 + PERM_CHUNK, :])
    half = ROW_TILE // 2
    for r in range(2):
        tok = slice(r * half, (r + 1) * half)
        ffn = lax.dot_general(wperm_scr[:, tok], ysel_scr[...], (((0,), (0,)), ((), ())),
                              preferred_element_type=F32)
        out_ref[tok, :] = _layer_norm(DEEPNORM_ALPHA * h_ref[tok, :] + ffn, g_ref[...], b_ref[...])


def _combine(meta, y_rows, h2, lp_t, gw_t, ln_g, ln_b):
    t, d = h2.shape
    nt = t // ROW_TILE
    grid_spec = pltpu.PrefetchScalarGridSpec(
        num_scalar_prefetch=3,
        grid=(nt,),
        in_specs=[
            pl.BlockSpec(memory_space=pl.ANY),
            pl.BlockSpec((ROW_TILE, d), lambda i, *_: (i, 0)),
            pl.BlockSpec((None, TOP_K, ROW_TILE), lambda i, *_: (i, 0, 0)),
            pl.BlockSpec((None, TOP_K, ROW_TILE), lambda i, *_: (i, 0, 0)),
            pl.BlockSpec((1, d), lambda i, *_: (0, 0)),
            pl.BlockSpec((1, d), lambda i, *_: (0, 0)),
        ],
        out_specs=pl.BlockSpec((ROW_TILE, d), lambda i, *_: (i, 0)),
        scratch_shapes=[pltpu.VMEM((2, PERM_ROWS, HALF_W), WORD), pltpu.VMEM((PERM_ROWS, ROW_TILE), BF16),
                        pltpu.VMEM((PERM_ROWS, d), BF16), pltpu.SemaphoreType.DMA((2,))],
    )
    return pl.pallas_call(
        _combine_kernel,
        grid_spec=grid_spec,
        out_shape=jax.ShapeDtypeStruct((t, d), F32),
        compiler_params=pltpu.CompilerParams(
            dimension_semantics=("arbitrary",), vmem_limit_bytes=VMEM_LIMIT),
        name="combine_ln",
    )(*meta, y_rows, h2, lp_t, gw_t, ln_g, ln_b)


def _max_blocks(t):
    nt = t // ROW_TILE
    rows = t * TOP_K + nt * N_EXPERTS * (RUN_ALIGN - 1) + N_EXPERTS * (MOE_BLOCK - RUN_ALIGN)
    return -(-rows // MOE_BLOCK)


def _layout(counts, nblk):
    n8 = (counts + RUN_ALIGN - 1) // RUN_ALIGN * RUN_ALIGN
    total = jnp.sum(n8, axis=0)
    padded = (total + MOE_BLOCK - 1) // MOE_BLOCK * MOE_BLOCK
    e_end = jnp.cumsum(padded)
    e_start = e_end - padded
    run_start = e_start[None, :] + jnp.cumsum(n8, axis=0) - n8
    off = jnp.cumsum(n8, axis=1) - n8
    blk_row0 = jnp.arange(nblk, dtype=I32) * MOE_BLOCK
    blk_e = jnp.minimum(jnp.sum((e_end[None, :] <= blk_row0[:, None]).astype(I32), axis=1), N_EXPERTS - 1)
    n_used = (e_end[-1] // MOE_BLOCK).reshape(1)
    meta = tuple(a.reshape(-1).astype(I32) for a in (run_start, n8, off))
    tails = ((e_start + total).astype(I32), (padded - total).astype(I32))
    return meta, tails, blk_e.astype(I32), n_used.astype(I32)


def kernel(x, w_in, attn_sinks, w_branch_a, w_branch_b, w_out, ln1_g, ln1_b, w_router, b_router,
           w_gate_up, b_gate_up, w_down, b_down, ln2_g, ln2_b):
    depth = w_in.shape[0]
    b, s, d = x.shape
    t = b * s
    nt = t // ROW_TILE

    pos = jnp.arange(s, dtype=F32)
    inv_freq = ROPE_THETA ** (-jnp.arange(0, HEAD_DIM, 2, dtype=F32) / HEAD_DIM)
    ang = pos[:, None] * inv_freq[None, :]
    cos32, sin32 = jnp.cos(ang), jnp.sin(ang)
    cos_t = jnp.tile(cos32, (1, LANES // (HEAD_DIM // 2)))
    sin_t = jnp.tile(jnp.concatenate([-sin32, sin32], axis=1), (1, LANES // HEAD_DIM))

    h = x
    for l in range(depth):
        wi = w_in[l]
        qscale = HEAD_DIM ** -0.5 * LOG2E
        cols = [wi[:, :A_Q_W] * qscale, wi[:, OFF_AK:OFF_BQ]]
        for g in range(len(B_GROUPS)):
            cols += [wi[:, OFF_BQ + g * B_GROUP_W:OFF_BQ + (g + 1) * B_GROUP_W] * qscale,
                     wi[:, OFF_BK + g * B_GROUP_W:OFF_BK + (g + 1) * B_GROUP_W],
                     wi[:, OFF_BV + g * B_GROUP_W:OFF_BV + (g + 1) * B_GROUP_W]]
        w_qkv = jnp.concatenate(cols, axis=1).astype(BF16)

        qkv_a, qkv_b0, qkv_b1, qkv_b2 = _qkv_rope(h, w_qkv, cos_t, sin_t)
        oa, ob = _attention(attn_sinks[l].astype(F32) * LOG2E, qkv_a, qkv_b0, qkv_b1, qkv_b2)

        wr = jnp.zeros((d, ROUTER_PAD), F32).at[:, :N_EXPERTS].set(w_router[l]).astype(BF16)
        br = jnp.full((1, ROUTER_PAD), NEG, F32).at[0, :N_EXPERTS].set(b_router[l].astype(F32))
        h1, lp_t, gw_t, counts = _merge(
            h.reshape(t, d), oa.reshape(t, A_Q_W), ob.reshape(t, B_GROUP_W),
            wi[:, OFF_G:].astype(BF16), w_branch_a[l].astype(BF16), w_branch_b[l].astype(BF16),
            w_out[l].astype(BF16), wr, br, ln1_g[l].reshape(1, d), ln1_b[l].reshape(1, d))

        nblk = _max_blocks(t)
        meta, tails, blk_e, n_used = _layout(counts.reshape(nt, ROUTER_PAD)[:, :N_EXPERTS].astype(I32), nblk)
        x_rows = _dispatch(meta, tails, n_used, h1, lp_t, nblk * MOE_BLOCK)
        y_rows = _expert_ffn(
            blk_e, n_used, x_rows,
            w_gate_up[l], b_gate_up[l].reshape(N_EXPERTS, 1, 2 * D_EXPERT),
            w_down[l], b_down[l].reshape(N_EXPERTS, 1, d))
        h = _combine(meta, y_rows, h1, lp_t, gw_t, ln2_g[l].reshape(1, d), ln2_b[l].reshape(1, d)).reshape(b, s, d)
    return h
```

```python
import jax
import jax.numpy as jnp
from jax import lax
from jax.experimental import pallas as pl
from jax.experimental.pallas import tpu as pltpu

F32 = jnp.float32
BF16 = jnp.bfloat16
I32 = jnp.int32
WORD = jnp.uint32

D_MODEL = 1024
HEAD_DIM = 64
ROPE_THETA = 10000.0
A_Q_W = 1024
A_KV_W = 256
A_KV_HEADS = 4
A_GQA = 4
A_WINDOW = 128
B_GROUPS = ((128, 1), (512, 4), (2048, 16))
B_GROUP_W = 256
B_HEADS = 4
B_W = 768
OFF_AK = A_Q_W
OFF_AV = OFF_AK + A_KV_W
OFF_BQ = OFF_AV + A_KV_W
OFF_BK = OFF_BQ + B_W
OFF_BV = OFF_BK + B_W
OFF_G = OFF_BV + B_W
N_EXPERTS = 32
TOP_K = 4
D_EXPERT = 1024
SWIGLU_LIMIT = 7.0
SWIGLU_ALPHA = 1.702
LN_EPS = 1e-5
DEEPNORM_ALPHA = 2.0 ** 0.25
LOG2E = 1.4426950408889634

LANES = 128
SUBLANES = 8
ATTN_BLOCK = 128
SEQ_TILE = 512
ROW_TILE = 512
MOE_BLOCK = 1024
FFN_CHUNK = 512
RUN_ALIGN = SUBLANES
PERM_ROWS = ROW_TILE * TOP_K + N_EXPERTS * RUN_ALIGN
PERM_CHUNK = 256
A_SEG_W = A_Q_W + 2 * A_KV_W
B_SEG_W = 3 * B_GROUP_W
QKV_W = A_SEG_W + 3 * B_SEG_W
HALF_W = D_MODEL // 2
ROUTER_PAD = LANES
NEG = -1e30
VMEM_LIMIT = 48 * 1024 * 1024
FFN_VMEM_LIMIT = 56 * 1024 * 1024


def _qkv_rope_kernel(x_ref, w_ref, cos_ref, sin_ref, a_ref, b0_ref, b1_ref, b2_ref, scr_ref):
    tm = x_ref.shape[0]
    xb = x_ref[...].astype(BF16)
    cos = cos_ref[...]
    sin = sin_ref[...]
    lane = lax.broadcasted_iota(I32, (tm, LANES), 1)
    first_half = (lane % HEAD_DIM) < (HEAD_DIM // 2)

    def rope(a):
        partner = jnp.where(first_half, pltpu.roll(a, LANES - 32, 1), pltpu.roll(a, 32, 1))
        return a * cos + partner * sin

    def segment(col0, width, n_rope_chunks):
        acc = jnp.dot(xb, w_ref[:, col0:col0 + width], preferred_element_type=F32)
        pieces = []
        for c in range(width // LANES):
            a = acc[:, c * LANES:(c + 1) * LANES]
            pieces.append(rope(a) if c < n_rope_chunks else a)
        return pieces

    a_ref[...] = jnp.concatenate(segment(0, A_SEG_W, (A_Q_W + A_KV_W) // LANES), axis=1).astype(BF16)
    b0_ref[...] = jnp.concatenate(segment(A_SEG_W, B_SEG_W, 2 * B_GROUP_W // LANES), axis=1).astype(BF16)
    n_chunks = B_SEG_W // LANES
    for out_ref, col0, dil in ((b1_ref, A_SEG_W + B_SEG_W, 4), (b2_ref, A_SEG_W + 2 * B_SEG_W, 16)):
        for k, piece in enumerate(segment(col0, B_SEG_W, 2 * B_GROUP_W // LANES)):
            scr_ref[k] = piece
        for c in range(dil):
            rows = [scr_ref[k, pl.ds(c, tm // dil, stride=dil), :] for k in range(n_chunks)]
            out_ref[c] = jnp.concatenate(rows, axis=1).astype(BF16)


def _qkv_rope(x, w_qkv, cos_t, sin_t):
    b, s, d = x.shape
    tm = SEQ_TILE
    grid = (b, s // tm)
    return pl.pallas_call(
        _qkv_rope_kernel,
        grid=grid,
        in_specs=[
            pl.BlockSpec((None, tm, d), lambda bi, si: (bi, si, 0)),
            pl.BlockSpec((d, QKV_W), lambda bi, si: (0, 0)),
            pl.BlockSpec((tm, LANES), lambda bi, si: (si, 0)),
            pl.BlockSpec((tm, LANES), lambda bi, si: (si, 0)),
        ],
        out_specs=[
            pl.BlockSpec((None, tm, A_SEG_W), lambda bi, si: (bi, si, 0)),
            pl.BlockSpec((None, tm, B_SEG_W), lambda bi, si: (bi, si, 0)),
            pl.BlockSpec((None, 4, tm // 4, B_SEG_W), lambda bi, si: (bi, 0, si, 0)),
            pl.BlockSpec((None, 16, tm // 16, B_SEG_W), lambda bi, si: (bi, 0, si, 0)),
        ],
        out_shape=[
            jax.ShapeDtypeStruct((b, s, A_SEG_W), BF16),
            jax.ShapeDtypeStruct((b, s, B_SEG_W), BF16),
            jax.ShapeDtypeStruct((b, 4, s // 4, B_SEG_W), BF16),
            jax.ShapeDtypeStruct((b, 16, s // 16, B_SEG_W), BF16),
        ],
        scratch_shapes=[pltpu.VMEM((B_SEG_W // LANES, tm, LANES), F32)],
        compiler_params=pltpu.CompilerParams(
            dimension_semantics=("arbitrary", "arbitrary"), vmem_limit_bytes=VMEM_LIMIT),
        name="qkv_rope",
    )(x, w_qkv, cos_t, sin_t)


def _stacked_heads_unit(q, kmat, vmat, bias, sink_slot=False):
    blk = q.shape[0]
    group = lax.broadcasted_iota(I32, (1, B_GROUP_W), 1) // HEAD_DIM
    zero = jnp.zeros_like(q)
    qs = jnp.concatenate([jnp.where(group == h, q, zero) for h in range(B_HEADS)], axis=0)
    if sink_slot:
        tile = 2 * SUBLANES
        not_slot = lax.broadcasted_iota(I32, (tile, B_GROUP_W), 0) > 0
        kmat = jnp.concatenate([jnp.where(not_slot, kmat[:tile], jnp.zeros_like(kmat[:tile])), kmat[tile:]], axis=0)
        vmat = jnp.concatenate([jnp.where(not_slot, vmat[:tile], jnp.zeros_like(vmat[:tile])), vmat[tile:]], axis=0)
    s = lax.dot_general(qs, kmat, (((1,), (1,)), ((), ())), preferred_element_type=F32) + bias
    m = jnp.max(s, axis=-1, keepdims=True)
    p = jnp.exp2(s - m)
    l = jnp.sum(p, axis=-1, keepdims=True)
    o = jnp.dot(p.astype(BF16), vmat, preferred_element_type=F32)
    out = jnp.zeros((blk, B_GROUP_W), F32)
    l_b = jnp.ones((blk, B_GROUP_W), F32)
    m_b = jnp.zeros((blk, B_GROUP_W), F32)
    for h in range(B_HEADS):
        out = jnp.where(group == h, o[h * blk:(h + 1) * blk], out)
        l_b = jnp.where(group == h, l[h * blk:(h + 1) * blk], l_b)
        m_b = jnp.where(group == h, m[h * blk:(h + 1) * blk], m_b)
    return out * (1.0 / l_b), m_b + jnp.log2(l_b)


def _store_lane_chunks(ref, lead, rows, x):
    for c in range(x.shape[1] // LANES):
        ref[(*lead, c, rows, slice(None))] = x[:, c * LANES:(c + 1) * LANES]


def _attention_kernel(sink_ref, a_cur, a_prev, b0_cur, b0_prev, b1_cur, b1_prev, b2_ref,
                      oa_ref, ob_ref, kexp_scr, vexp_scr, kvb_scr, o_scr, lse_scr, o2_scr, lse2_scr,
                      band_scr, sinkband_scr, causal_scr):
    j = pl.program_id(1)
    blk = ATTN_BLOCK
    nqb = SEQ_TILE // blk
    stack = B_HEADS * blk

    @pl.when(jnp.logical_and(pl.program_id(0) == 0, j == 0))
    def _():
        row = lax.broadcasted_iota(I32, (stack, 2 * blk), 0) % blk
        col = lax.broadcasted_iota(I32, (stack, 2 * blk), 1)
        row_block = lax.broadcasted_iota(I32, (stack, 2 * blk), 0) // blk
        dist = blk + row - col
        in_band = (dist >= 0) & (dist <= B_GROUPS[0][0] // B_GROUPS[0][1])
        band_scr[0] = jnp.where(in_band, 0.0, NEG)
        band_scr[1] = jnp.where(in_band & (col >= blk), 0.0, NEG)
        in_band = (dist >= 0) & (dist <= A_WINDOW - 1)
        for kvh in range(A_KV_HEADS):
            sink = jnp.zeros((stack, 2 * blk), F32)
            for g in range(A_GQA):
                sink = jnp.where(row_block == g, sink_ref[kvh * A_GQA + g], sink)
            sinkband_scr[2 * kvh] = jnp.where(col == 0, sink, jnp.where(in_band, 0.0, NEG))
            sinkband_scr[2 * kvh + 1] = jnp.where(col == 0, sink, jnp.where(in_band & (col >= blk), 0.0, NEG))
        r1 = lax.broadcasted_iota(I32, (stack, blk), 0) % blk
        c1 = lax.broadcasted_iota(I32, (stack, blk), 1)
        causal_scr[...] = jnp.where(c1 <= r1, 0.0, NEG)

    @pl.when(j == 0)
    def _():
        def class_body(c, carry):
            t = b2_ref[c]
            out, lse_b = _stacked_heads_unit(t[:, :B_GROUP_W], t[:, B_GROUP_W:2 * B_GROUP_W],
                                             t[:, 2 * B_GROUP_W:], causal_scr[...])
            rows = pl.ds(c, blk, stride=16)
            _store_lane_chunks(o2_scr, (), rows, out)
            _store_lane_chunks(lse2_scr, (), rows, lse_b)
            return carry

        lax.fori_loop(0, 16, class_body, 0, unroll=8)

    for src, r0, r1 in ((a_prev, 0, blk), (a_cur, blk, blk + SEQ_TILE)):
        for dst, col0 in ((kexp_scr, A_Q_W), (vexp_scr, A_Q_W + A_KV_W)):
            heads = src[:, col0:col0 + A_KV_W]
            for kvh in range(A_KV_HEADS):
                head = heads[:, kvh * HEAD_DIM:(kvh + 1) * HEAD_DIM]
                dst[r0:r1, kvh * B_GROUP_W:(kvh + 1) * B_GROUP_W] = jnp.concatenate([head] * A_GQA, axis=1)
    kvb_scr[0:blk, :] = b0_prev[:, B_GROUP_W:]
    kvb_scr[blk:, :] = b0_cur[:, B_GROUP_W:]

    def qblock_body(qb, carry):
        r0 = pl.multiple_of(qb * blk, blk)
        first = jnp.logical_and(j == 0, qb == 0).astype(I32)
        for kvh in range(A_KV_HEADS):
            lanes = slice(kvh * B_GROUP_W, (kvh + 1) * B_GROUP_W)
            out, _ = _stacked_heads_unit(a_cur[pl.ds(r0, blk), lanes], kexp_scr[pl.ds(r0, 2 * blk), lanes],
                                         vexp_scr[pl.ds(r0, 2 * blk), lanes], sinkband_scr[2 * kvh + first],
                                         sink_slot=True)
            oa_ref[pl.ds(r0, blk), lanes] = out.astype(oa_ref.dtype)
        out, lse_b = _stacked_heads_unit(b0_cur[pl.ds(r0, blk), 0:B_GROUP_W],
                                         kvb_scr[pl.ds(r0, 2 * blk), 0:B_GROUP_W],
                                         kvb_scr[pl.ds(r0, 2 * blk), B_GROUP_W:], band_scr[first])
        _store_lane_chunks(o_scr, (0,), pl.ds(r0, blk), out)
        _store_lane_chunks(lse_scr, (0,), pl.ds(r0, blk), lse_b)
        return carry

    lax.fori_loop(0, nqb, qblock_body, 0, unroll=True)

    first_class_block = (j == 0).astype(I32)

    def class4_body(c, carry):
        cur = b1_cur[c]
        prev = b1_prev[c]
        kv = jnp.concatenate([prev[:, B_GROUP_W:], cur[:, B_GROUP_W:]], axis=0)
        out, lse_b = _stacked_heads_unit(cur[:, :B_GROUP_W], kv[:, :B_GROUP_W], kv[:, B_GROUP_W:],
                                         band_scr[first_class_block])
        rows = pl.ds(c, blk, stride=4)
        _store_lane_chunks(o_scr, (1,), rows, out)
        _store_lane_chunks(lse_scr, (1,), rows, lse_b)
        return carry

    lax.fori_loop(0, 4, class4_body, 0, unroll=True)

    t0 = pl.multiple_of(j * SEQ_TILE, SEQ_TILE)
    for c in range(B_GROUP_W // LANES):
        l0, l1, l2 = lse_scr[0, c], lse_scr[1, c], lse2_scr[c, pl.ds(t0, SEQ_TILE), :]
        mx = jnp.maximum(jnp.maximum(l0, l1), l2)
        w0, w1, w2 = jnp.exp2(l0 - mx), jnp.exp2(l1 - mx), jnp.exp2(l2 - mx)
        num = w0 * o_scr[0, c] + w1 * o_scr[1, c] + w2 * o2_scr[c, pl.ds(t0, SEQ_TILE), :]
        ob_ref[:, c * LANES:(c + 1) * LANES] = (num / (w0 + w1 + w2)).astype(ob_ref.dtype)


def _attention(sinks, qkv_a, qkv_b0, qkv_b1, qkv_b2):
    b, s, _ = qkv_a.shape
    t = SEQ_TILE
    blk = ATTN_BLOCK
    nq = t // blk
    grid = (b, s // t)
    n_chunks = B_GROUP_W // LANES
    return pl.pallas_call(
        _attention_kernel,
        grid=grid,
        in_specs=[
            pl.BlockSpec(memory_space=pltpu.SMEM),
            pl.BlockSpec((None, t, A_SEG_W), lambda bi, j: (bi, j, 0)),
            pl.BlockSpec((None, blk, A_SEG_W), lambda bi, j: (bi, jnp.maximum(j * nq - 1, 0), 0)),
            pl.BlockSpec((None, t, B_SEG_W), lambda bi, j: (bi, j, 0)),
            pl.BlockSpec((None, blk, B_SEG_W), lambda bi, j: (bi, jnp.maximum(j * nq - 1, 0), 0)),
            pl.BlockSpec((None, 4, blk, B_SEG_W), lambda bi, j: (bi, 0, j, 0)),
            pl.BlockSpec((None, 4, blk, B_SEG_W), lambda bi, j: (bi, 0, jnp.maximum(j - 1, 0), 0)),
            pl.BlockSpec((None, 16, blk, B_SEG_W), lambda bi, j: (bi, 0, 0, 0)),
        ],
        out_specs=[
            pl.BlockSpec((None, t, A_Q_W), lambda bi, j: (bi, j, 0)),
            pl.BlockSpec((None, t, B_GROUP_W), lambda bi, j: (bi, j, 0)),
        ],
        out_shape=[
            jax.ShapeDtypeStruct((b, s, A_Q_W), BF16),
            jax.ShapeDtypeStruct((b, s, B_GROUP_W), BF16),
        ],
        scratch_shapes=[
            pltpu.VMEM((blk + t, A_KV_HEADS * B_GROUP_W), BF16),
            pltpu.VMEM((blk + t, A_KV_HEADS * B_GROUP_W), BF16),
            pltpu.VMEM((blk + t, 2 * B_GROUP_W), BF16),
            pltpu.VMEM((2, n_chunks, t, LANES), F32),
            pltpu.VMEM((2, n_chunks, t, LANES), F32),
            pltpu.VMEM((n_chunks, s, LANES), F32),
            pltpu.VMEM((n_chunks, s, LANES), F32),
            pltpu.VMEM((2, B_HEADS * blk, 2 * blk), F32),
            pltpu.VMEM((2 * A_KV_HEADS, B_HEADS * blk, 2 * blk), F32),
            pltpu.VMEM((B_HEADS * blk, blk), F32),
        ],
        compiler_params=pltpu.CompilerParams(
            dimension_semantics=("arbitrary", "arbitrary"), vmem_limit_bytes=VMEM_LIMIT),
        name="banded_attention",
    )(sinks, qkv_a, qkv_a, qkv_b0, qkv_b0, qkv_b1, qkv_b1, qkv_b2)


def _layer_norm(z, g, b):
    mu = jnp.mean(z, axis=-1, keepdims=True)
    zc = z - mu
    var = jnp.mean(zc * zc, axis=-1, keepdims=True)
    return zc * lax.rsqrt(var + LN_EPS) * g + b


def _merge_kernel(x_ref, oa_ref, ob_ref, wg_ref, wa_ref, wb_ref, wo_ref, wr_ref, br_ref, g_ref, b_ref,
                  h_ref, lp_ref, gw_ref, cnt_ref, logits_scr):
    i = pl.program_id(0)

    @pl.when(i == 0)
    def _():
        logits_scr[...] = jnp.zeros_like(logits_scr)

    logits = logits_scr[(i + 1) % 2]

    x = x_ref[...]
    xb = x.astype(BF16)
    gates = jax.nn.sigmoid(jnp.dot(xb, wg_ref[...], preferred_element_type=F32))
    ya = jnp.dot(oa_ref[...], wa_ref[...], preferred_element_type=F32)
    yb = jnp.dot(ob_ref[...], wb_ref[...], preferred_element_type=F32)
    merged = gates[:, :D_MODEL] * ya + gates[:, D_MODEL:] * yb
    mix = jnp.dot(merged.astype(BF16), wo_ref[...], preferred_element_type=F32)
    h = _layer_norm(DEEPNORM_ALPHA * x + mix, g_ref[...], b_ref[...])
    h_ref[...] = h
    logits_scr[i % 2] = jnp.dot(h.astype(BF16), wr_ref[...], preferred_element_type=F32) + br_ref[...]

    tm = logits.shape[0]
    lane = lax.broadcasted_iota(I32, (tm, ROUTER_PAD), 1)
    lane_f = lane.astype(F32)
    v_out = jnp.full((tm, ROUTER_PAD), NEG, F32)
    onehots = []
    lg = logits
    for k in range(TOP_K):
        m = jnp.max(lg, axis=-1, keepdims=True)
        idx = jnp.min(jnp.where(lg == m, lane_f, float(ROUTER_PAD)), axis=-1, keepdims=True)
        chosen = lane_f == idx
        onehots.append(chosen.astype(F32))
        v_out = jnp.where(lane == k, m, v_out)
        lg = jnp.where(chosen, 3.0 * NEG, lg)
    vmax = jnp.max(v_out, axis=-1, keepdims=True)
    ex = jnp.exp(v_out - vmax)
    gw_ref[...] = jnp.transpose(ex / jnp.sum(ex, axis=-1, keepdims=True))[:TOP_K]

    routed = onehots[0] + onehots[1] + onehots[2] + onehots[3]
    r_i = lax.broadcasted_iota(I32, (tm, tm), 0)
    c_i = lax.broadcasted_iota(I32, (tm, tm), 1)
    before = jnp.dot((c_i < r_i).astype(BF16), routed.astype(BF16), preferred_element_type=F32)
    cnt = jnp.sum(routed, axis=0, keepdims=True)
    runs = jnp.ceil(cnt * (1.0 / RUN_ALIGN))
    e_r = lax.broadcasted_iota(I32, (ROUTER_PAD, ROUTER_PAD), 0)
    e_c = lax.broadcasted_iota(I32, (ROUTER_PAD, ROUTER_PAD), 1)
    off = jnp.dot(jnp.broadcast_to(runs, (SUBLANES, ROUTER_PAD)).astype(BF16), (e_r < e_c).astype(BF16),
                  preferred_element_type=F32)[0:1] * float(RUN_ALIGN)
    base = before + off
    lp = jnp.zeros((tm, ROUTER_PAD), F32)
    for k in range(TOP_K):
        lp = jnp.where(lane == k, jnp.sum(onehots[k] * base, axis=-1, keepdims=True), lp)
    lp_ref[...] = jnp.transpose(lp)[:TOP_K].astype(I32)
    cnt_ref[...] = cnt


def _merge(x2, oa2, ob2, wg, wa, wb, wo, wr, br, ln_g, ln_b):
    t, d = x2.shape
    tm = ROW_TILE
    nt = t // tm
    full = lambda shape: pl.BlockSpec(shape, lambda i: (0, 0))
    dense = lambda i: (jnp.minimum(i, nt - 1), 0)
    routed = lambda i: (jnp.maximum(i - 1, 0), 0, 0)
    return pl.pallas_call(
        _merge_kernel,
        grid=(nt + 1,),
        in_specs=[
            pl.BlockSpec((tm, d), dense),
            pl.BlockSpec((tm, A_Q_W), dense),
            pl.BlockSpec((tm, B_GROUP_W), dense),
            full((d, 2 * d)), full((A_Q_W, d)), full((B_GROUP_W, d)), full((d, d)),
            full((d, ROUTER_PAD)), full((1, ROUTER_PAD)), full((1, d)), full((1, d)),
        ],
        out_specs=[
            pl.BlockSpec((tm, d), dense),
            pl.BlockSpec((None, TOP_K, tm), routed),
            pl.BlockSpec((None, TOP_K, tm), routed),
            pl.BlockSpec((None, 1, ROUTER_PAD), routed),
        ],
        out_shape=[
            jax.ShapeDtypeStruct((t, d), F32),
            jax.ShapeDtypeStruct((nt, TOP_K, tm), I32),
            jax.ShapeDtypeStruct((nt, TOP_K, tm), F32),
            jax.ShapeDtypeStruct((nt, 1, ROUTER_PAD), F32),
        ],
        scratch_shapes=[pltpu.VMEM((2, tm, ROUTER_PAD), F32)],
        compiler_params=pltpu.CompilerParams(
            dimension_semantics=("arbitrary",), vmem_limit_bytes=VMEM_LIMIT),
        name="merge_ln_router",
    )(x2, oa2, ob2, wg, wa, wb, wo, wr, br, ln_g, ln_b)


def _pack_rows(v, already_bf16=False):
    vb = v if already_bf16 else v.astype(BF16).astype(F32)
    lo = lax.bitcast_convert_type(vb[:, :HALF_W], WORD) >> 16
    hi = lax.bitcast_convert_type(vb[:, HALF_W:], WORD) & jnp.uint32(0xFFFF0000)
    return lo | hi


def _unpack_rows(w):
    lo = lax.bitcast_convert_type(w << 16, F32)
    hi = lax.bitcast_convert_type(w & jnp.uint32(0xFFFF0000), F32)
    return jnp.concatenate([lo, hi], axis=1).astype(BF16)


def _run_copy(tile, e, meta, vmem_rows, hbm_rows, sem, to_hbm):
    run_start_ref, n8_ref, off_ref = meta
    idx = tile * N_EXPERTS + e
    n = pl.multiple_of(n8_ref[idx], RUN_ALIGN)
    local = vmem_rows.at[pl.ds(pl.multiple_of(off_ref[idx], RUN_ALIGN), n)]
    remote = hbm_rows.at[pl.ds(pl.multiple_of(run_start_ref[idx], RUN_ALIGN), n)]
    return (pltpu.make_async_copy(local, remote, sem) if to_hbm
            else pltpu.make_async_copy(remote, local, sem)), n


def _start_runs(tile, meta, vmem_rows, hbm_rows, sem, to_hbm):
    for e in range(N_EXPERTS):
        copy, n = _run_copy(tile, e, meta, vmem_rows, hbm_rows, sem, to_hbm)

        @pl.when(n > 0)
        def _(copy=copy, e=e):
            copy.start(priority=e % 2)


def _wait_runs(tile, meta, vmem_rows, hbm_rows, sem, to_hbm):
    _, n8_ref, off_ref = meta
    last = tile * N_EXPERTS + N_EXPERTS - 1
    total = pl.multiple_of(off_ref[last] + n8_ref[last], RUN_ALIGN)
    local = vmem_rows.at[pl.ds(0, total)]
    remote = hbm_rows.at[pl.ds(0, total)]
    (pltpu.make_async_copy(local, remote, sem) if to_hbm else pltpu.make_async_copy(remote, local, sem)).wait()


def _dispatch_kernel(run_start_ref, n8_ref, off_ref, tail_start_ref, tail_len_ref, nused_ref,
                     h_ref, lpt_ref, x_hbm, buf, zbuf, sem, zsem):
    i = pl.program_id(0)
    nt = pl.num_programs(0)
    slot = i % 2
    meta = (run_start_ref, n8_ref, off_ref)

    @pl.when(i >= 2)
    def _():
        _wait_runs(i - 2, meta, buf.at[slot], x_hbm, sem.at[slot], True)

    hb = h_ref[...].astype(BF16)
    lp16 = lpt_ref[...].astype(jnp.int16)
    one = jnp.ones((PERM_CHUNK, ROW_TILE), BF16)
    for c in range(PERM_ROWS // PERM_CHUNK):
        r0 = c * PERM_CHUNK
        rows = (lax.broadcasted_iota(I32, (PERM_CHUNK, ROW_TILE), 0) + r0).astype(jnp.int16)
        perm = jnp.zeros((PERM_CHUNK, ROW_TILE), BF16)
        for k in range(TOP_K):
            perm = jnp.where(rows == lp16[k:k + 1, :], one, perm)
        buf[slot, r0:r0 + PERM_CHUNK, :] = _pack_rows(jnp.dot(perm, hb, preferred_element_type=F32),
                                                      already_bf16=True)

    _start_runs(i, meta, buf.at[slot], x_hbm, sem.at[slot], True)

    @pl.when(i == nt - 1)
    def _():
        zbuf[...] = jnp.zeros_like(zbuf)

        def tail_copy(e):
            n = pl.multiple_of(tail_len_ref[e], RUN_ALIGN)
            dst = x_hbm.at[pl.ds(pl.multiple_of(tail_start_ref[e], RUN_ALIGN), n)]
            return pltpu.make_async_copy(zbuf.at[pl.ds(0, n)], dst, zsem), n

        def start_tail(e, carry):
            copy, n = tail_copy(e)

            @pl.when(n > 0)
            def _():
                copy.start()
            return carry

        def wait_tail(e, carry):
            copy, n = tail_copy(e)

            @pl.when(n > 0)
            def _():
                copy.wait()
            return carry

        lax.fori_loop(0, N_EXPERTS, start_tail, 0)
        lax.fori_loop(0, N_EXPERTS, wait_tail, 0)

        def spare_copy(blk_i):
            dst = x_hbm.at[pl.ds(pl.multiple_of(blk_i * MOE_BLOCK, MOE_BLOCK), MOE_BLOCK)]
            return pltpu.make_async_copy(zbuf, dst, zsem)

        def start_spare(blk_i, carry):
            spare_copy(blk_i).start()
            return carry

        def wait_spare(blk_i, carry):
            spare_copy(blk_i).wait()
            return carry

        n_blocks = x_hbm.shape[0] // MOE_BLOCK
        lax.fori_loop(nused_ref[0], n_blocks, start_spare, 0)
        lax.fori_loop(nused_ref[0], n_blocks, wait_spare, 0)

        @pl.when(nt >= 2)
        def _():
            _wait_runs(i - 1, meta, buf.at[1 - slot], x_hbm, sem.at[1 - slot], True)
        _wait_runs(i, meta, buf.at[slot], x_hbm, sem.at[slot], True)


def _dispatch(meta, tails, n_used, h2, lp_t, n_rows):
    t, d = h2.shape
    nt = t // ROW_TILE
    grid_spec = pltpu.PrefetchScalarGridSpec(
        num_scalar_prefetch=6,
        grid=(nt,),
        in_specs=[
            pl.BlockSpec((ROW_TILE, d), lambda i, *_: (i, 0)),
            pl.BlockSpec((None, TOP_K, ROW_TILE), lambda i, *_: (i, 0, 0)),
        ],
        out_specs=pl.BlockSpec(memory_space=pl.ANY),
        scratch_shapes=[
            pltpu.VMEM((2, PERM_ROWS, HALF_W), WORD),
            pltpu.VMEM((MOE_BLOCK, HALF_W), WORD),
            pltpu.SemaphoreType.DMA((2,)),
            pltpu.SemaphoreType.DMA(()),
        ],
    )
    return pl.pallas_call(
        _dispatch_kernel,
        grid_spec=grid_spec,
        out_shape=jax.ShapeDtypeStruct((n_rows, HALF_W), WORD),
        compiler_params=pltpu.CompilerParams(
            dimension_semantics=("arbitrary",), vmem_limit_bytes=VMEM_LIMIT, has_side_effects=True),
        name="dispatch",
    )(*meta, *tails, n_used, h2, lp_t)


def _ffn_kernel(blk_e_ref, nused_ref, x_ref, wgu_ref, bgu_ref, wdn_ref, bdn_ref, y_ref, wgu_bf, wdn_bf):
    i = pl.program_id(0)
    used = i < nused_ref[0]
    new_expert = jnp.logical_or(i == 0, blk_e_ref[i] != blk_e_ref[jnp.maximum(i - 1, 0)])

    @pl.when(jnp.logical_and(used, new_expert))
    def _():
        wgu_bf[...] = wgu_ref[...].astype(BF16)
        wdn_bf[...] = wdn_ref[...].astype(BF16)

    @pl.when(used)
    def _():
        xb = _unpack_rows(x_ref[...])
        acc = jnp.zeros((MOE_BLOCK, D_MODEL), F32)
        for c in range(D_EXPERT // FFN_CHUNK):
            lo = c * FFN_CHUNK
            gate = jnp.dot(xb, wgu_bf[:, lo:lo + FFN_CHUNK], preferred_element_type=F32)
            gate = gate + bgu_ref[:, lo:lo + FFN_CHUNK]
            up = jnp.dot(xb, wgu_bf[:, D_EXPERT + lo:D_EXPERT + lo + FFN_CHUNK], preferred_element_type=F32)
            up = up + bgu_ref[:, D_EXPERT + lo:D_EXPERT + lo + FFN_CHUNK]
            gate = jnp.minimum(gate, SWIGLU_LIMIT)
            up = jnp.clip(up, -SWIGLU_LIMIT, SWIGLU_LIMIT)
            hmid = (up + 1.0) * gate * jax.nn.sigmoid(SWIGLU_ALPHA * gate)
            acc = acc + jnp.dot(hmid.astype(BF16), wdn_bf[lo:lo + FFN_CHUNK, :], preferred_element_type=F32)
        y_ref[...] = _pack_rows(acc + bdn_ref[...])

    @pl.when(i >= nused_ref[0])
    def _():
        y_ref[...] = jnp.zeros_like(y_ref)


def _expert_ffn(blk_e, n_used, x_rows, wgu, bgu, wdn, bdn):
    nblk = x_rows.shape[0] // MOE_BLOCK
    d = D_MODEL
    grid_spec = pltpu.PrefetchScalarGridSpec(
        num_scalar_prefetch=2,
        grid=(nblk,),
        in_specs=[
            pl.BlockSpec((MOE_BLOCK, HALF_W), lambda i, be, nu: (jnp.minimum(i, nu[0] - 1), 0)),
            pl.BlockSpec((None, d, 2 * D_EXPERT), lambda i, be, nu: (be[i], 0, 0)),
            pl.BlockSpec((None, 1, 2 * D_EXPERT), lambda i, be, nu: (be[i], 0, 0)),
            pl.BlockSpec((None, D_EXPERT, d), lambda i, be, nu: (be[i], 0, 0)),
            pl.BlockSpec((None, 1, d), lambda i, be, nu: (be[i], 0, 0)),
        ],
        out_specs=pl.BlockSpec((MOE_BLOCK, HALF_W), lambda i, be, nu: (i, 0)),
        scratch_shapes=[pltpu.VMEM((d, 2 * D_EXPERT), BF16), pltpu.VMEM((D_EXPERT, d), BF16)],
    )
    return pl.pallas_call(
        _ffn_kernel,
        grid_spec=grid_spec,
        out_shape=jax.ShapeDtypeStruct((nblk * MOE_BLOCK, HALF_W), WORD),
        compiler_params=pltpu.CompilerParams(
            dimension_semantics=("arbitrary",), vmem_limit_bytes=FFN_VMEM_LIMIT),
        name="expert_ffn",
    )(blk_e, n_used, x_rows, wgu, bgu, wdn, bdn)


def _combine_kernel(run_start_ref, n8_ref, off_ref, y_hbm, h_ref, lpt_ref, gwt_ref, g_ref, b_ref,
                    out_ref, ybuf, wperm_scr, ysel_scr, sem):
    i = pl.program_id(0)
    nt = pl.num_programs(0)
    slot = i % 2
    meta = (run_start_ref, n8_ref, off_ref)

    @pl.when(i == 0)
    def _():
        ybuf[...] = jnp.zeros_like(ybuf)
        _start_runs(0, meta, ybuf.at[0], y_hbm, sem.at[0], False)

    @pl.when(i + 1 < nt)
    def _():
        _start_runs(i + 1, meta, ybuf.at[1 - slot], y_hbm, sem.at[1 - slot], False)

    _wait_runs(i, meta, ybuf.at[slot], y_hbm, sem.at[slot], False)

    lp16 = lpt_ref[...].astype(jnp.int16)
    gw16 = gwt_ref[...].astype(BF16)
    for c in range(PERM_ROWS // PERM_CHUNK):
        r0 = c * PERM_CHUNK
        rows = (lax.broadcasted_iota(I32, (PERM_CHUNK, ROW_TILE), 0) + r0).astype(jnp.int16)
        wperm = jnp.zeros((PERM_CHUNK, ROW_TILE), BF16)
        for k in range(TOP_K):
            wperm = jnp.where(rows == lp16[k:k + 1, :], gw16[k:k + 1, :], wperm)
        wperm_scr[r0:r0 + PERM_CHUNK, :] = wperm
        ysel_scr[r0:r0 + PERM_CHUNK, :] = _unpack_rows(ybuf[slot, r0:r0 + PERM_CHUNK, :])
    half = ROW_TILE // 2
    for r in range(2):
        tok = slice(r * half, (r + 1) * half)
        ffn = lax.dot_general(wperm_scr[:, tok], ysel_scr[...], (((0,), (0,)), ((), ())),
                              preferred_element_type=F32)
        out_ref[tok, :] = _layer_norm(DEEPNORM_ALPHA * h_ref[tok, :] + ffn, g_ref[...], b_ref[...])


def _combine(meta, y_rows, h2, lp_t, gw_t, ln_g, ln_b):
    t, d = h2.shape
    nt = t // ROW_TILE
    grid_spec = pltpu.PrefetchScalarGridSpec(
        num_scalar_prefetch=3,
        grid=(nt,),
        in_specs=[
            pl.BlockSpec(memory_space=pl.ANY),
            pl.BlockSpec((ROW_TILE, d), lambda i, *_: (i, 0)),
            pl.BlockSpec((None, TOP_K, ROW_TILE), lambda i, *_: (i, 0, 0)),
            pl.BlockSpec((None, TOP_K, ROW_TILE), lambda i, *_: (i, 0, 0)),
            pl.BlockSpec((1, d), lambda i, *_: (0, 0)),
            pl.BlockSpec((1, d), lambda i, *_: (0, 0)),
        ],
        out_specs=pl.BlockSpec((ROW_TILE, d), lambda i, *_: (i, 0)),
        scratch_shapes=[pltpu.VMEM((2, PERM_ROWS, HALF_W), WORD), pltpu.VMEM((PERM_ROWS, ROW_TILE), BF16),
                        pltpu.VMEM((PERM_ROWS, d), BF16), pltpu.SemaphoreType.DMA((2,))],
    )
    return pl.pallas_call(
        _combine_kernel,
        grid_spec=grid_spec,
        out_shape=jax.ShapeDtypeStruct((t, d), F32),
        compiler_params=pltpu.CompilerParams(
            dimension_semantics=("arbitrary",), vmem_limit_bytes=VMEM_LIMIT),
        name="combine_ln",
    )(*meta, y_rows, h2, lp_t, gw_t, ln_g, ln_b)


def _max_blocks(t):
    nt = t // ROW_TILE
    rows = t * TOP_K + nt * N_EXPERTS * (RUN_ALIGN - 1) + N_EXPERTS * (MOE_BLOCK - RUN_ALIGN)
    return -(-rows // MOE_BLOCK)


def _layout(counts, nblk):
    n8 = (counts + RUN_ALIGN - 1) // RUN_ALIGN * RUN_ALIGN
    total = jnp.sum(n8, axis=0)
    padded = (total + MOE_BLOCK - 1) // MOE_BLOCK * MOE_BLOCK
    e_end = jnp.cumsum(padded)
    e_start = e_end - padded
    run_start = e_start[None, :] + jnp.cumsum(n8, axis=0) - n8
    off = jnp.cumsum(n8, axis=1) - n8
    blk_row0 = jnp.arange(nblk, dtype=I32) * MOE_BLOCK
    blk_e = jnp.minimum(jnp.sum((e_end[None, :] <= blk_row0[:, None]).astype(I32), axis=1), N_EXPERTS - 1)
    n_used = (e_end[-1] // MOE_BLOCK).reshape(1)
    meta = tuple(a.reshape(-1).astype(I32) for a in (run_start, n8, off))
    tails = ((e_start + total).astype(I32), (padded - total).astype(I32))
    return meta, tails, blk_e.astype(I32), n_used.astype(I32)


def kernel(x, w_in, attn_sinks, w_branch_a, w_branch_b, w_out, ln1_g, ln1_b, w_router, b_router,
           w_gate_up, b_gate_up, w_down, b_down, ln2_g, ln2_b):
    depth = w_in.shape[0]
    b, s, d = x.shape
    t = b * s
    nt = t // ROW_TILE

    pos = jnp.arange(s, dtype=F32)
    inv_freq = ROPE_THETA ** (-jnp.arange(0, HEAD_DIM, 2, dtype=F32) / HEAD_DIM)
    ang = pos[:, None] * inv_freq[None, :]
    cos32, sin32 = jnp.cos(ang), jnp.sin(ang)
    cos_t = jnp.tile(cos32, (1, LANES // (HEAD_DIM // 2)))
    sin_t = jnp.tile(jnp.concatenate([-sin32, sin32], axis=1), (1, LANES // HEAD_DIM))

    h = x
    for l in range(depth):
        wi = w_in[l]
        qscale = HEAD_DIM ** -0.5 * LOG2E
        cols = [wi[:, :A_Q_W] * qscale, wi[:, OFF_AK:OFF_BQ]]
        for g in range(len(B_GROUPS)):
            cols += [wi[:, OFF_BQ + g * B_GROUP_W:OFF_BQ + (g + 1) * B_GROUP_W] * qscale,
                     wi[:, OFF_BK + g * B_GROUP_W:OFF_BK + (g + 1) * B_GROUP_W],
                     wi[:, OFF_BV + g * B_GROUP_W:OFF_BV + (g + 1) * B_GROUP_W]]
        w_qkv = jnp.concatenate(cols, axis=1).astype(BF16)

        qkv_a, qkv_b0, qkv_b1, qkv_b2 = _qkv_rope(h, w_qkv, cos_t, sin_t)
        oa, ob = _attention(attn_sinks[l].astype(F32) * LOG2E, qkv_a, qkv_b0, qkv_b1, qkv_b2)

        wr = jnp.zeros((d, ROUTER_PAD), F32).at[:, :N_EXPERTS].set(w_router[l]).astype(BF16)
        br = jnp.full((1, ROUTER_PAD), NEG, F32).at[0, :N_EXPERTS].set(b_router[l].astype(F32))
        h1, lp_t, gw_t, counts = _merge(
            h.reshape(t, d), oa.reshape(t, A_Q_W), ob.reshape(t, B_GROUP_W),
            wi[:, OFF_G:].astype(BF16), w_branch_a[l].astype(BF16), w_branch_b[l].astype(BF16),
            w_out[l].astype(BF16), wr, br, ln1_g[l].reshape(1, d), ln1_b[l].reshape(1, d))

        nblk = _max_blocks(t)
        meta, tails, blk_e, n_used = _layout(counts.reshape(nt, ROUTER_PAD)[:, :N_EXPERTS].astype(I32), nblk)
        x_rows = _dispatch(meta, tails, n_used, h1, lp_t, nblk * MOE_BLOCK)
        y_rows = _expert_ffn(
            blk_e, n_used, x_rows,
            w_gate_up[l], b_gate_up[l].reshape(N_EXPERTS, 1, 2 * D_EXPERT),
            w_down[l], b_down[l].reshape(N_EXPERTS, 1, d))
        h = _combine(meta, y_rows, h1, lp_t, gw_t, ln2_g[l].reshape(1, d), ln2_b[l].reshape(1, d)).reshape(b, s, d)
    return h
```
